```python
import jax, jax.numpy as jnp
from jax import lax
import numpy as np

D_MODEL = 1024
BATCH = 4
SEQ = 8192
DEPTH = 2

D_MIX = D_MODEL
SC_WIDTH = D_MIX // 4
SC_GROUPS = 4
SC_KERNEL = 3
GDN_WIDTH = D_MIX // 2
GDN_HEADS = 4
GDN_HEAD_DIM = GDN_WIDTH // GDN_HEADS
GDN_CONV = 4
GDN_CHUNK = 64
SB_WIDTH = D_MIX - SC_WIDTH - GDN_WIDTH
SB_HEADS = 4
SB_HEAD_DIM = SB_WIDTH // SB_HEADS
SB_BLOCK = 128
D_FF = 256 * ((8 * D_MODEL // 3 + 255) // 256)
FFN_CONV = 3
NORM_EPS = 1e-6
SPLIT_SIZES = (SC_WIDTH, SC_WIDTH, SC_WIDTH,
               GDN_WIDTH, GDN_WIDTH, GDN_WIDTH, GDN_WIDTH, GDN_HEADS, GDN_HEADS,
               SB_WIDTH, SB_WIDTH, SB_WIDTH)
D_IN_PROJ = 3 * SC_WIDTH + 4 * GDN_WIDTH + 2 * GDN_HEADS + 3 * SB_WIDTH

kernel_name = 'hymba_style_conv_gdn_stickbreaking_hybrid'


def rmsnorm(x, w):
    xf = x.astype(jnp.float32)
    y = xf * lax.rsqrt(jnp.mean(xf * xf, axis=-1, keepdims=True) + NORM_EPS) * w.astype(jnp.float32)
    return y.astype(x.dtype)


def l2norm(x):
    xf = x.astype(jnp.float32)
    return xf * lax.rsqrt(jnp.sum(xf * xf, axis=-1, keepdims=True) + NORM_EPS)


def causal_dwconv(x, w):
    K, C = w.shape
    return lax.conv_general_dilated(
        x, w[:, None, :].astype(x.dtype), window_strides=(1,), padding=[(K - 1, 0)],
        dimension_numbers=('NWC', 'WIO', 'NWC'), feature_group_count=C)


def split_columns(proj):
    points = [int(p) for p in np.cumsum(np.array(SPLIT_SIZES))[:-1]]
    return jnp.split(proj, points, axis=-1)


def gated_delta_rule_chunked(q, k, v, g, beta):
    f32 = jnp.float32
    Bsz, L, H, Dk = q.shape
    Dv = v.shape[-1]
    N = L // GDN_CHUNK

    def to_chunks(t):
        t = t.reshape((Bsz, N, GDN_CHUNK, H) + t.shape[3:])
        return jnp.moveaxis(t, 3, 1)

    q = to_chunks(q.astype(f32)) * (Dk ** -0.5)
    k = to_chunks(k.astype(f32))
    v = to_chunks(v.astype(f32))
    beta = to_chunks(beta.astype(f32))
    g = jnp.cumsum(to_chunks(g.astype(f32)), axis=-1)
    idx = jnp.arange(GDN_CHUNK)
    causal = idx[:, None] >= idx[None, :]
    strict = idx[:, None] > idx[None, :]
    decay = jnp.exp(jnp.where(causal, g[..., :, None] - g[..., None, :], -jnp.inf))
    k_beta = k * beta[..., None]
    lower = jnp.where(strict, jnp.einsum('bhncd,bhnsd->bhncs', k_beta, k) * decay, 0.0)
    lhs = lower + jnp.eye(GDN_CHUNK, dtype=f32)
    rhs = jnp.concatenate([v * beta[..., None], k_beta * jnp.exp(g)[..., None]], axis=-1)
    sol = lax.linalg.triangular_solve(lhs, rhs, left_side=True, lower=True, unit_diagonal=True)
    u, w = sol[..., :Dv], sol[..., Dv:]
    attn_intra = jnp.where(causal, jnp.einsum('bhncd,bhnsd->bhncs', q, k) * decay, 0.0)

    def step(S, inp):
        q_i, k_i, u_i, w_i, g_i, a_i = inp
        v_new = u_i - jnp.einsum('bhcd,bhde->bhce', w_i, S)
        o = jnp.einsum('bhcd,bhde->bhce', q_i * jnp.exp(g_i)[..., None], S) + \
            jnp.einsum('bhcs,bhse->bhce', a_i, v_new)
        g_last = g_i[..., -1]
        S = S * jnp.exp(g_last)[..., None, None] + \
            jnp.einsum('bhcd,bhce->bhde', k_i * jnp.exp(g_last[..., None] - g_i)[..., None], v_new)
        return S, o

    xs = tuple(jnp.moveaxis(t, 2, 0) for t in (q, k, u, w, g, attn_intra))
    S0 = jnp.zeros((Bsz, H, Dk, Dv), f32)
    _, outs = lax.scan(step, S0, xs)
    return outs.transpose(1, 0, 3, 2, 4).reshape(Bsz, L, H, Dv)


def stick_breaking_attention(q, k, v):
    L, D = q.shape[1], q.shape[-1]
    scale = D ** -0.5
    q_off = jnp.arange(SB_BLOCK)
    outs = []
    for blk in range(L // SB_BLOCK):
        start, end = blk * SB_BLOCK, (blk + 1) * SB_BLOCK
        z = jnp.einsum('bqhd,bkhd->bhqk', q[:, start:end], k[:, :end]).astype(jnp.float32) * scale
        strict = jnp.arange(end)[None, :] < (start + q_off)[:, None]
        log_beta = jax.nn.log_sigmoid(z)
        log_one_minus = jnp.where(strict, jax.nn.log_sigmoid(-z), 0.0)
        tail = lax.cumsum(log_one_minus, axis=3, reverse=True) - log_one_minus
        A = jnp.where(strict, jnp.exp(log_beta + tail), 0.0)
        outs.append(jnp.einsum('bhqk,bkhd->bqhd', A.astype(v.dtype), v[:, :end]))
    return jnp.concatenate(outs, axis=1)


def setup_inputs(seed: int = 0) -> dict:
    key = jax.random.key(seed)
    ks = jax.random.split(key, 16)
    f32 = jnp.float32
    nrm = lambda k, shape: jax.random.normal(k, shape, f32)
    dt = jnp.exp(jax.random.uniform(ks[6], (DEPTH, GDN_HEADS), f32, np.log(1e-3), np.log(1e-1)))
    return {
        'x': nrm(ks[0], (BATCH, SEQ, D_MODEL)),
        'w_norm_mix': 1.0 + 0.02 * nrm(ks[1], (DEPTH, D_MODEL)),
        'w_mix_in': nrm(ks[2], (DEPTH, D_MODEL, D_IN_PROJ)) * D_MODEL ** -0.5,
        'w_sconv': nrm(ks[3], (DEPTH, SC_KERNEL, SC_WIDTH)) * SC_KERNEL ** -0.5,
        'w_gdn_conv': nrm(ks[4], (DEPTH, GDN_CONV, 3 * GDN_WIDTH)) * GDN_CONV ** -0.5,
        'gdn_a_log': jnp.log(jax.random.uniform(ks[5], (DEPTH, GDN_HEADS), f32, 1.0, 16.0)),
        'gdn_dt_bias': dt + jnp.log(-jnp.expm1(-dt)),
        'w_gdn_norm': 1.0 + 0.02 * nrm(ks[7], (DEPTH, GDN_HEAD_DIM)),
        'w_mix_out': nrm(ks[8], (DEPTH, D_MIX, D_MODEL)) * D_MIX ** -0.5,
        'w_norm_ffn': 1.0 + 0.02 * nrm(ks[9], (DEPTH, D_MODEL)),
        'w_ffn_up': nrm(ks[10], (DEPTH, D_MODEL, 2 * D_FF)) * D_MODEL ** -0.5,
        'w_ffn_conv': nrm(ks[11], (DEPTH, FFN_CONV, 2 * D_FF)) * FFN_CONV ** -0.5,
        'w_ffn_down': nrm(ks[12], (DEPTH, D_FF, D_MODEL)) * D_FF ** -0.5,
        'w_norm_final': 1.0 + 0.02 * nrm(ks[13], (D_MODEL,)),
    }


def reference(x, w_norm_mix, w_mix_in, w_sconv, w_gdn_conv, gdn_a_log, gdn_dt_bias, w_gdn_norm,
              w_mix_out, w_norm_ffn, w_ffn_up, w_ffn_conv, w_ffn_down, w_norm_final):
    Bsz, L, _ = x.shape
    f32 = jnp.float32
    for l in range(DEPTH):
        h = rmsnorm(x, w_norm_mix[l])
        proj = h @ w_mix_in[l]
        (sc_b, sc_c, sc_h, gq, gk, gv, gz, ga, gb, sq, sk, sv) = split_columns(proj)

        y_sc = sc_b * causal_dwconv(sc_c * sc_h, w_sconv[l])

        qkv = jax.nn.silu(causal_dwconv(jnp.concatenate([gq, gk, gv], axis=-1), w_gdn_conv[l]))
        gq, gk, gv = jnp.split(qkv, 3, axis=-1)
        heads = lambda t: t.reshape(Bsz, L, GDN_HEADS, GDN_HEAD_DIM)
        beta = jax.nn.sigmoid(gb.astype(f32))
        g = -jnp.exp(gdn_a_log[l].astype(f32)) * jax.nn.softplus(ga.astype(f32) + gdn_dt_bias[l].astype(f32))
        o = gated_delta_rule_chunked(l2norm(heads(gq)), l2norm(heads(gk)), heads(gv), g, beta)
        o = rmsnorm(o, w_gdn_norm[l]) * jax.nn.silu(heads(gz).astype(f32))
        y_gdn = o.reshape(Bsz, L, GDN_WIDTH).astype(x.dtype)

        sb_heads = lambda t: t.reshape(Bsz, L, SB_HEADS, SB_HEAD_DIM)
        y_sb = stick_breaking_attention(sb_heads(sq), sb_heads(sk), sb_heads(sv)).reshape(Bsz, L, SB_WIDTH)

        x = x + jnp.concatenate([y_sc, y_gdn, y_sb], axis=-1) @ w_mix_out[l]

        h = rmsnorm(x, w_norm_ffn[l])
        u = causal_dwconv(h @ w_ffn_up[l], w_ffn_conv[l])
        gate, val = jnp.split(u, 2, axis=-1)
        x = x + (jax.nn.silu(gate) * val) @ w_ffn_down[l]
    return rmsnorm(x, w_norm_final)
```

```python
import functools

import jax
import jax.numpy as jnp
from jax import lax
from jax.experimental import pallas as pl
from jax.experimental.pallas import tpu as pltpu

F32 = jnp.float32
BF16 = jnp.bfloat16
NORM_EPS = 1e-6
HIGHEST = lax.Precision.HIGHEST

LANES = 128
SUBLANES = 8
VMEM_LIMIT = 56 * 1024 * 1024

SC_KERNEL = 3
GDN_HEADS = 4
GDN_CONV = 4
GDN_CHUNK = 64
SB_HEADS = 4
FFN_CONV = 3
SB_SKIP_LOG = -88.0

_NT = (((1,), (1,)), ((), ()))
_TN = (((0,), (0,)), ((), ()))


def _rms(x, w):
    return x * lax.rsqrt(jnp.mean(x * x, axis=-1, keepdims=True) + NORM_EPS) * w


def _softplus(x):
    return jnp.maximum(x, 0.0) + jnp.log1p(jnp.exp(-jnp.abs(x)))


def _sigmoid(x):
    return 1.0 / (1.0 + jnp.exp(-x))


def _silu(x):
    return x * _sigmoid(x)


def _const_spec(shape):
    nd = len(shape)
    return pl.BlockSpec(shape, lambda *_: (0,) * nd, pipeline_mode=pl.Buffered(1))


def _prev_rows_map(rows_per_tile):
    step = rows_per_tile // SUBLANES
    return lambda i: (jnp.maximum(i * step - 1, 0), 0)


def _inproj_body(x_ref, wn_ref, wsc_ref, wg_ref, wsb_ref, wab_ref, osc_ref, og_ref, osb_ref, oab_ref):
    h = _rms(x_ref[...], wn_ref[...]).astype(BF16)
    osc_ref[...] = jnp.dot(h, wsc_ref[...], preferred_element_type=F32)
    og_ref[...] = jnp.dot(h, wg_ref[...], preferred_element_type=F32)
    osb_ref[...] = jnp.dot(h, wsb_ref[...], preferred_element_type=F32).astype(BF16)
    oab_ref[...] = jnp.dot(h, wab_ref[...], preferred_element_type=F32)


def _inproj(x2, wn, wsc, wg, wsb, wab, tm):
    T, D = x2.shape
    outs = (wsc.shape[1], wg.shape[1], wsb.shape[1], wab.shape[1])
    return pl.pallas_call(
        _inproj_body,
        grid=(T // tm,),
        in_specs=[pl.BlockSpec((tm, D), lambda i: (i, 0)), _const_spec(wn.shape), _const_spec(wsc.shape),
                  _const_spec(wg.shape), _const_spec(wsb.shape), _const_spec(wab.shape)],
        out_specs=[pl.BlockSpec((tm, n), lambda i: (i, 0)) for n in outs],
        out_shape=[jax.ShapeDtypeStruct((T, outs[0]), F32), jax.ShapeDtypeStruct((T, outs[1]), F32),
                   jax.ShapeDtypeStruct((T, outs[2]), BF16), jax.ShapeDtypeStruct((T, outs[3]), F32)],
        compiler_params=pltpu.CompilerParams(dimension_semantics=("arbitrary",), vmem_limit_bytes=VMEM_LIMIT),
        name="inproj",
    )(x2, wn, wsc, wg, wsb, wab)


def _unit_lower_inverse(lm):
    n = lm.shape[0]
    eye = (lax.broadcasted_iota(jnp.int32, (n, n), 0) == lax.broadcasted_iota(jnp.int32, (n, n), 1)).astype(F32)
    dot = functools.partial(jnp.dot, preferred_element_type=F32, precision=HIGHEST)
    p = eye - lm
    m = lm
    span = 2
    while span < n:
        m = dot(m, m)
        p = p + dot(p, m)
        span *= 2
    return p


def _gdn_body(g_ref, gprev_ref, ab_ref, wconv_ref, alog_ref, dtb_ref, wnorm_ref, y_ref, s_ref, qkv_ref,
              *, lt, width, heads):
    hd = width // heads
    C = GDN_CHUNK
    t = pl.program_id(1)

    @pl.when(t == 0)
    def _():
        s_ref[...] = jnp.zeros_like(s_ref)

    keep = jnp.where(t > 0, 1.0, 0.0).astype(F32)
    wconv = wconv_ref[...]
    for s in range(3 * heads):
        cols = slice(s * hd, (s + 1) * hd)
        xp = jnp.concatenate([gprev_ref[:, cols] * keep, g_ref[:, cols]], axis=0)
        conv = xp[SUBLANES:] * wconv[GDN_CONV - 1:GDN_CONV, cols]
        for k in range(1, GDN_CONV):
            conv = conv + xp[SUBLANES - k:SUBLANES - k + lt] * wconv[GDN_CONV - 1 - k:GDN_CONV - k, cols]
        a = _silu(conv)
        if s < 2 * heads:
            a = a * lax.rsqrt(jnp.sum(a * a, axis=-1, keepdims=True) + NORM_EPS)
        if s < heads:
            a = a * (hd ** -0.5)
        qkv_ref[:, cols] = a

    ri = lax.broadcasted_iota(jnp.int32, (C, C), 0)
    ci = lax.broadcasted_iota(jnp.int32, (C, C), 1)
    causal = ri >= ci
    strict = ri > ci
    ltri = causal.astype(F32)
    neg_a = -jnp.exp(alog_ref[...])
    dtb = dtb_ref[...]
    wnorm = wnorm_ref[...]

    def chunk(c, carry):
        r0 = pl.multiple_of(c * C, C)
        rows = pl.ds(r0, C)
        ab = ab_ref[rows, :]
        g_all = neg_a * _softplus(ab + dtb)
        beta_all = _sigmoid(ab)
        gc_all = jnp.dot(ltri, g_all, preferred_element_type=F32, precision=HIGHEST)
        gc_t = gc_all.T
        eg_all = jnp.exp(gc_all)
        g_last = gc_all[C - 1:C, :]
        ekd_all = jnp.exp(g_last - gc_all)
        eg_last = jnp.exp(g_last)
        for h in range(heads):
            q = qkv_ref[rows, h * hd:(h + 1) * hd]
            k = qkv_ref[rows, (heads + h) * hd:(heads + h + 1) * hd]
            v = qkv_ref[rows, (2 * heads + h) * hd:(2 * heads + h + 1) * hd]
            beta = beta_all[:, heads + h:heads + h + 1]
            diff = gc_all[:, h:h + 1] - gc_t[h:h + 1, :]
            decay = jnp.where(causal, jnp.exp(jnp.where(causal, diff, 0.0)), 0.0)
            kb = k * beta
            kk = lax.dot_general(kb, k, _NT, preferred_element_type=F32)
            tinv = _unit_lower_inverse(jnp.where(strict, kk * decay, 0.0))
            rhs = jnp.concatenate([v * beta, kb * eg_all[:, h:h + 1]], axis=1)
            sol = jnp.dot(tinv, rhs, preferred_element_type=F32, precision=HIGHEST)
            u = sol[:, :hd]
            w = sol[:, hd:]
            attn = lax.dot_general(q, k, _NT, preferred_element_type=F32) * decay
            state = s_ref[h]
            wq = jnp.concatenate([w, q * eg_all[:, h:h + 1]], axis=0)
            wq_s = jnp.dot(wq, state, preferred_element_type=F32)
            v_new = u - wq_s[:C]
            o = wq_s[C:] + jnp.dot(attn, v_new, preferred_element_type=F32)
            s_ref[h] = state * eg_last[:, h:h + 1] + lax.dot_general(
                k * ekd_all[:, h:h + 1], v_new, _TN, preferred_element_type=F32)
            z = g_ref[rows, (3 * heads + h) * hd:(3 * heads + h + 1) * hd]
            y_ref[rows, h * hd:(h + 1) * hd] = _rms(o, wnorm) * _silu(z)
        return carry

    lax.fori_loop(0, lt // C, chunk, 0)


def _gdn(og, oab, wconv, alog, dtb, wnorm, batch, seq, lt):
    T, gw = og.shape
    width = gw // 4
    heads = GDN_HEADS
    nt = seq // lt
    body = functools.partial(_gdn_body, lt=lt, width=width, heads=heads)
    step = lt // SUBLANES
    return pl.pallas_call(
        body,
        grid=(batch, nt),
        in_specs=[pl.BlockSpec((lt, gw), lambda b, t: (b * nt + t, 0)),
                  pl.BlockSpec((SUBLANES, 3 * width), lambda b, t: (jnp.maximum((b * nt + t) * step - 1, 0), 0)),
                  pl.BlockSpec((lt, LANES), lambda b, t: (b * nt + t, 0)),
                  pl.BlockSpec(wconv.shape, lambda b, t: (0, 0)),
                  pl.BlockSpec(alog.shape, lambda b, t: (0, 0)),
                  pl.BlockSpec(dtb.shape, lambda b, t: (0, 0)),
                  pl.BlockSpec(wnorm.shape, lambda b, t: (0, 0))],
        out_specs=pl.BlockSpec((lt, width), lambda b, t: (b * nt + t, 0)),
        out_shape=jax.ShapeDtypeStruct((T, width), F32),
        scratch_shapes=[pltpu.VMEM((heads, width // heads, width // heads), F32),
                        pltpu.VMEM((lt, 3 * width), F32)],
        compiler_params=pltpu.CompilerParams(dimension_semantics=("arbitrary", "arbitrary"),
                                             vmem_limit_bytes=VMEM_LIMIT),
        name="gdn",
    )(og, og, oab, wconv, alog, dtb, wnorm)


def _sb_body(q_ref, k_ref, v_ref, o_ref, *, tq, hd):
    i = pl.program_id(2)
    scale = hd ** -0.5
    lane = lax.broadcasted_iota(jnp.int32, (1, LANES), 1)
    q = q_ref[...]
    zero = jnp.zeros_like(q)
    qq = jnp.concatenate([jnp.where(lane < hd, q, zero), jnp.where(lane >= hd, q, zero)], axis=0)
    row = lax.broadcasted_iota(jnp.int32, (2 * tq, tq), 0)
    qpos = jnp.where(row >= tq, row - tq, row)
    strict = lax.broadcasted_iota(jnp.int32, (2 * tq, tq), 1) < qpos
    later = (lax.broadcasted_iota(jnp.int32, (tq, tq), 0) > lax.broadcasted_iota(jnp.int32, (tq, tq), 1)).astype(BF16)

    def block(j, carry, acc, diagonal):
        rows = pl.ds(pl.multiple_of(j * tq, tq), tq)
        z = lax.dot_general(qq, k_ref[rows, :], _NT, preferred_element_type=F32) * scale
        lom = -_softplus(z)
        if diagonal:
            lom = jnp.where(strict, lom, 0.0)
        hi = lom.astype(BF16)
        lo = (lom - hi.astype(F32)).astype(BF16)
        tail = jnp.dot(hi, later, preferred_element_type=F32) + jnp.dot(lo, later, preferred_element_type=F32)
        a = jnp.exp(z + lom + tail + carry)
        if diagonal:
            a = jnp.where(strict, a, 0.0)
        acc = acc + jnp.dot(a.astype(BF16), v_ref[rows, :], preferred_element_type=F32)
        carry = carry + jnp.sum(lom, axis=-1, keepdims=True)
        return carry, acc

    carry, acc = block(i, jnp.zeros((2 * tq, 1), F32), jnp.zeros((2 * tq, LANES), F32), True)

    def cond(state):
        j, carry, _ = state
        return jnp.logical_and(j >= 0, jnp.max(carry) > SB_SKIP_LOG)

    def body(state):
        j, carry, acc = state
        carry, acc = block(j, carry, acc, False)
        return j - 1, carry, acc

    _, _, acc = lax.while_loop(cond, body, (i - 1, carry, acc))
    o_ref[...] = jnp.where(lane < hd, acc[:tq], acc[tq:])


def _sb(osb, batch, seq, tq):
    T, w3 = osb.shape
    width = w3 // 3
    pairs = width // LANES
    hd = width // SB_HEADS
    nq = seq // tq
    body = functools.partial(_sb_body, tq=tq, hd=hd)
    return pl.pallas_call(
        body,
        grid=(batch, pairs, nq),
        in_specs=[pl.BlockSpec((tq, LANES), lambda b, p, i: (b * nq + i, p)),
                  pl.BlockSpec((seq, LANES), lambda b, p, i: (b, pairs + p)),
                  pl.BlockSpec((seq, LANES), lambda b, p, i: (b, 2 * pairs + p))],
        out_specs=pl.BlockSpec((tq, LANES), lambda b, p, i: (b * nq + i, p)),
        out_shape=jax.ShapeDtypeStruct((T, width), F32),
        compiler_params=pltpu.CompilerParams(dimension_semantics=("arbitrary", "arbitrary", "arbitrary"),
                                             vmem_limit_bytes=VMEM_LIMIT),
        name="stickbreak",
    )(osb, osb, osb)


def _mixout_body(x_ref, sc_ref, scprev_ref, yg_ref, ysb_ref, wsconv_ref, wa_ref, wb_ref, wc_ref, o_ref,
                 *, tm, tiles_per_seq, scw):
    i = pl.program_id(0)
    keep = jnp.where(i % tiles_per_seq > 0, 1.0, 0.0).astype(F32)
    p = sc_ref[:, scw:2 * scw] * sc_ref[:, 2 * scw:3 * scw]
    pprev = scprev_ref[:, scw:2 * scw] * scprev_ref[:, 2 * scw:3 * scw] * keep
    pp = jnp.concatenate([pprev, p], axis=0)
    w = wsconv_ref[...]
    conv = pp[SUBLANES:] * w[SC_KERNEL - 1:SC_KERNEL]
    for k in range(1, SC_KERNEL):
        conv = conv + pp[SUBLANES - k:SUBLANES - k + tm] * w[SC_KERNEL - 1 - k:SC_KERNEL - k]
    ysc = sc_ref[:, 0:scw] * conv
    acc = jnp.dot(ysc.astype(BF16), wa_ref[...], preferred_element_type=F32)
    acc = acc + jnp.dot(yg_ref[...].astype(BF16), wb_ref[...], preferred_element_type=F32)
    acc = acc + jnp.dot(ysb_ref[...].astype(BF16), wc_ref[...], preferred_element_type=F32)
    o_ref[...] = x_ref[...] + acc


def _mixout(x2, osc, yg, ysb, wsconv, wa, wb, wc, seq, tm):
    T, D = x2.shape
    scw = wa.shape[0]
    body = functools.partial(_mixout_body, tm=tm, tiles_per_seq=seq // tm, scw=scw)
    row = lambda n: pl.BlockSpec((tm, n), lambda i: (i, 0))
    return pl.pallas_call(
        body,
        grid=(T // tm,),
        in_specs=[row(D), row(osc.shape[1]), pl.BlockSpec((SUBLANES, osc.shape[1]), _prev_rows_map(tm)),
                  row(yg.shape[1]), row(ysb.shape[1]), _const_spec(wsconv.shape),
                  _const_spec(wa.shape), _const_spec(wb.shape), _const_spec(wc.shape)],
        out_specs=row(D),
        out_shape=jax.ShapeDtypeStruct((T, D), F32),
        compiler_params=pltpu.CompilerParams(dimension_semantics=("arbitrary",), vmem_limit_bytes=VMEM_LIMIT),
        name="mixout",
    )(x2, osc, osc, yg, ysb, wsconv, wa, wb, wc)


def _ffn_body(x_ref, xprev_ref, wn_ref, wup_ref, wconv_ref, wdown_ref, wfin_ref, o_ref, acc_ref,
              *, tm, tiles_per_seq, dff, cw, final_norm):
    i = pl.program_id(0)
    keep = jnp.where(i % tiles_per_seq > 0, 1.0, 0.0).astype(F32)
    x = x_ref[...]
    xp = jnp.concatenate([xprev_ref[...] * keep, x], axis=0)
    h = _rms(xp, wn_ref[...]).astype(BF16)

    def conv(u, cols):
        w = wconv_ref[:, cols]
        out = u[SUBLANES:] * w[FFN_CONV - 1:FFN_CONV]
        for k in range(1, FFN_CONV):
            out = out + u[SUBLANES - k:SUBLANES - k + tm] * w[FFN_CONV - 1 - k:FFN_CONV - k]
        return out

    for j in range(dff // cw):
        gcols = slice(j * cw, (j + 1) * cw)
        vcols = slice(dff + j * cw, dff + (j + 1) * cw)
        gate = conv(jnp.dot(h, wup_ref[:, gcols], preferred_element_type=F32), gcols)
        val = conv(jnp.dot(h, wup_ref[:, vcols], preferred_element_type=F32), vcols)
        part = jnp.dot((_silu(gate) * val).astype(BF16), wdown_ref[gcols, :], preferred_element_type=F32)
        if j == 0:
            acc_ref[...] = part
        else:
            acc_ref[...] += part
    y = x + acc_ref[...]
    if final_norm:
        y = _rms(y, wfin_ref[...])
    o_ref[...] = y


def _ffn(x2, wn, wup, wconv, wdown, wfin, seq, tm, final_norm):
    T, D = x2.shape
    dff = wdown.shape[0]
    body = functools.partial(_ffn_body, tm=tm, tiles_per_seq=seq // tm, dff=dff, cw=256, final_norm=final_norm)
    return pl.pallas_call(
        body,
        grid=(T // tm,),
        in_specs=[pl.BlockSpec((tm, D), lambda i: (i, 0)), pl.BlockSpec((SUBLANES, D), _prev_rows_map(tm)),
                  _const_spec(wn.shape), _const_spec(wup.shape), _const_spec(wconv.shape),
                  _const_spec(wdown.shape), _const_spec(wfin.shape)],
        out_specs=pl.BlockSpec((tm, D), lambda i: (i, 0)),
        out_shape=jax.ShapeDtypeStruct((T, D), F32),
        scratch_shapes=[pltpu.VMEM((tm, D), F32)],
        compiler_params=pltpu.CompilerParams(dimension_semantics=("arbitrary",), vmem_limit_bytes=VMEM_LIMIT),
        name="ffn",
    )(x2, x2, wn, wup, wconv, wdown, wfin)


def _pad_lanes(a):
    return jnp.pad(a, ((0, 0), (0, LANES - a.shape[1])))


def kernel(x, w_norm_mix, w_mix_in, w_sconv, w_gdn_conv, gdn_a_log, gdn_dt_bias, w_gdn_norm, w_mix_out,
           w_norm_ffn, w_ffn_up, w_ffn_conv, w_ffn_down, w_norm_final):
    batch, seq, D = x.shape
    depth = w_mix_in.shape[0]
    scw = D // 4
    gw = D // 2
    sbw = D - scw - gw
    c_g = 3 * scw
    c_ab = c_g + 4 * gw
    c_sb = c_ab + 2 * GDN_HEADS
    assert w_mix_in.shape[2] == c_sb + 3 * sbw and gw // GDN_HEADS == LANES and sbw % LANES == 0

    x2 = x.reshape(batch * seq, D).astype(F32)
    row = lambda v: v.reshape(1, -1).astype(F32)
    for l in range(depth):
        w_in = w_mix_in[l]
        wsc = w_in[:, :c_g].astype(BF16)
        wg = w_in[:, c_g:c_ab].astype(BF16)
        wab = _pad_lanes(w_in[:, c_ab:c_sb]).astype(BF16)
        wsb = w_in[:, c_sb:].astype(BF16)
        osc, og, osb, oab = _inproj(x2, row(w_norm_mix[l]), wsc, wg, wsb, wab, tm=512)

        yg = _gdn(og, oab, w_gdn_conv[l].astype(F32), _pad_lanes(row(gdn_a_log[l])), _pad_lanes(row(gdn_dt_bias[l])),
                  row(w_gdn_norm[l]), batch, seq, lt=256)
        ysb = _sb(osb, batch, seq, tq=128)

        w_out = w_mix_out[l].astype(BF16)
        x2 = _mixout(x2, osc, yg, ysb, w_sconv[l].astype(F32), w_out[:scw], w_out[scw:scw + gw], w_out[scw + gw:],
                     seq, tm=512)
        x2 = _ffn(x2, row(w_norm_ffn[l]), w_ffn_up[l].astype(BF16), w_ffn_conv[l].astype(F32),
                  w_ffn_down[l].astype(BF16), row(w_norm_final), seq, tm=256, final_norm=(l == depth - 1))
    return x2.reshape(batch, seq, D).astype(x.dtype)
```

```python
import functools

import jax
import jax.numpy as jnp
from jax import lax
from jax.experimental import pallas as pl
from jax.experimental.pallas import tpu as pltpu

F32 = jnp.float32
BF16 = jnp.bfloat16
NORM_EPS = 1e-6
HIGHEST = lax.Precision.HIGHEST

LANES = 128
SUBLANES = 8
VMEM_LIMIT = 56 * 1024 * 1024

SC_KERNEL = 3
GDN_HEADS = 4
GDN_CONV = 4
GDN_CHUNK = 64
SB_HEADS = 4
SB_BLOCK = 128
FFN_CONV = 3
SB_SKIP_LOG = -88.0

_NT = (((1,), (1,)), ((), ()))
_TN = (((0,), (0,)), ((), ()))


def _rms(x, w):
    return x * lax.rsqrt(jnp.mean(x * x, axis=-1, keepdims=True) + NORM_EPS) * w


def _softplus(x):
    return jnp.maximum(x, 0.0) + jnp.log1p(jnp.exp(-jnp.abs(x)))


def _sigmoid(x):
    return 1.0 / (1.0 + jnp.exp(-x))


def _silu(x):
    return x * _sigmoid(x)


def _const_spec(shape):
    nd = len(shape)
    return pl.BlockSpec(shape, lambda *_: (0,) * nd, pipeline_mode=pl.Buffered(1))


def _prev_rows_map(rows_per_tile):
    step = rows_per_tile // SUBLANES
    return lambda i: (jnp.maximum(i * step - 1, 0), 0)


def _inproj_body(x_ref, wn_ref, wsc_ref, wg_ref, wsb_ref, wab_ref, osc_ref, og_ref, osb_ref, oab_ref):
    h = _rms(x_ref[...], wn_ref[...]).astype(BF16)
    osc_ref[...] = jnp.dot(h, wsc_ref[...], preferred_element_type=F32)
    og_ref[...] = jnp.dot(h, wg_ref[...], preferred_element_type=F32)
    osb_ref[...] = jnp.dot(h, wsb_ref[...], preferred_element_type=F32).astype(BF16)
    oab_ref[...] = jnp.dot(h, wab_ref[...], preferred_element_type=F32)


def _inproj(x2, wn, wsc, wg, wsb, wab, tm):
    T, D = x2.shape
    outs = (wsc.shape[1], wg.shape[1], wsb.shape[1], wab.shape[1])
    return pl.pallas_call(
        _inproj_body,
        grid=(T // tm,),
        in_specs=[pl.BlockSpec((tm, D), lambda i: (i, 0)), _const_spec(wn.shape), _const_spec(wsc.shape),
                  _const_spec(wg.shape), _const_spec(wsb.shape), _const_spec(wab.shape)],
        out_specs=[pl.BlockSpec((tm, n), lambda i: (i, 0)) for n in outs],
        out_shape=[jax.ShapeDtypeStruct((T, outs[0]), F32), jax.ShapeDtypeStruct((T, outs[1]), F32),
                   jax.ShapeDtypeStruct((T, outs[2]), BF16), jax.ShapeDtypeStruct((T, outs[3]), F32)],
        compiler_params=pltpu.CompilerParams(dimension_semantics=("arbitrary",), vmem_limit_bytes=VMEM_LIMIT),
        name="inproj",
    )(x2, wn, wsc, wg, wsb, wab)


def _split2(x):
    hi = x.astype(BF16)
    lo = (x - hi.astype(F32)).astype(BF16)
    return hi, lo


def _dot_split(a, b):
    ah, al = a
    bh, bl = b
    dot = functools.partial(jnp.dot, preferred_element_type=F32)
    return dot(ah, bh) + (dot(ah, bl) + dot(al, bh))


def _gdn_body(g_ref, gprev_ref, ab_ref, wconv_ref, alog_ref, dtb_ref, wnorm_ref, y_ref, s_ref, qkv_ref,
              *, lt, width, heads):
    hd = width // heads
    C = GDN_CHUNK
    nc = lt // C
    t = pl.program_id(1)

    @pl.when(t == 0)
    def _():
        s_ref[...] = jnp.zeros_like(s_ref)

    keep = jnp.where(t > 0, 1.0, 0.0).astype(F32)
    wconv = wconv_ref[...]
    for s in range(3 * heads):
        cols = slice(s * hd, (s + 1) * hd)
        xp = jnp.concatenate([gprev_ref[:, cols] * keep, g_ref[:, cols]], axis=0)
        conv = xp[SUBLANES:] * wconv[GDN_CONV - 1:GDN_CONV, cols]
        for k in range(1, GDN_CONV):
            conv = conv + xp[SUBLANES - k:SUBLANES - k + lt] * wconv[GDN_CONV - 1 - k:GDN_CONV - k, cols]
        a = _silu(conv)
        if s < 2 * heads:
            a = a * lax.rsqrt(jnp.sum(a * a, axis=-1, keepdims=True) + NORM_EPS)
        if s < heads:
            a = a * (hd ** -0.5)
        qkv_ref[:, cols] = a

    ri = lax.broadcasted_iota(jnp.int32, (C, C), 0)
    ci = lax.broadcasted_iota(jnp.int32, (C, C), 1)
    causal = ri >= ci
    strict = ri > ci
    eye = (ri == ci).astype(F32)
    ltri = causal.astype(BF16)
    neg_a = -jnp.exp(alog_ref[...])
    dtb = dtb_ref[...]
    wnorm = wnorm_ref[...]
    dot = functools.partial(jnp.dot, preferred_element_type=F32)

    tiles = [(c, h) for c in range(nc) for h in range(heads)]
    gate = []
    for c in range(nc):
        rows = slice(c * C, (c + 1) * C)
        ab = ab_ref[rows, :]
        g_all = neg_a * _softplus(ab + dtb)
        g1 = g_all.astype(BF16)
        r1 = g_all - g1.astype(F32)
        g2 = r1.astype(BF16)
        g3 = (r1 - g2.astype(F32)).astype(BF16)
        gc_all = dot(ltri, g1) + (dot(ltri, g2) + dot(ltri, g3))
        g_last = gc_all[C - 1:C, :]
        gate.append(dict(beta=_sigmoid(ab), gc=gc_all, gc_t=gc_all.T, eg=jnp.exp(gc_all),
                         ekd=jnp.exp(g_last - gc_all), eg_last=jnp.exp(g_last)))

    ks, qs, decays, kbs, rhss = [], [], [], [], []
    for c, h in tiles:
        gt = gate[c]
        rows = slice(c * C, (c + 1) * C)
        q = qkv_ref[rows, h * hd:(h + 1) * hd]
        k = qkv_ref[rows, (heads + h) * hd:(heads + h + 1) * hd]
        v = qkv_ref[rows, (2 * heads + h) * hd:(2 * heads + h + 1) * hd]
        beta = gt["beta"][:, heads + h:heads + h + 1]
        diff = gt["gc"][:, h:h + 1] - gt["gc_t"][h:h + 1, :]
        decays.append(jnp.where(causal, jnp.exp(jnp.where(causal, diff, 0.0)), 0.0))
        kb = k * beta
        ks.append(k)
        qs.append(q)
        kbs.append(kb)
        rhss.append(jnp.concatenate([v * beta, kb * gt["eg"][:, h:h + 1]], axis=1))
    kk = [lax.dot_general(kb.astype(BF16), k.astype(BF16), _NT, preferred_element_type=F32) for kb, k in zip(kbs, ks)]
    qk = [lax.dot_general(q.astype(BF16), k.astype(BF16), _NT, preferred_element_type=F32) for q, k in zip(qs, ks)]
    lms = [jnp.where(strict, x * d, 0.0) for x, d in zip(kk, decays)]
    attns = [(x * d).astype(BF16) for x, d in zip(qk, decays)]
    ms = [_split2(lm) for lm in lms]
    ps = [eye - lm for lm in lms]
    span = 2
    while span < C:
        ms = [_split2(_dot_split(m, m)) for m in ms]
        ps = [p + _dot_split(_split2(p), m) for p, m in zip(ps, ms)]
        span *= 2
    sols = [_dot_split(_split2(p), _split2(r)) for p, r in zip(ps, rhss)]

    for c in range(nc):
        gt = gate[c]
        rows = slice(c * C, (c + 1) * C)
        idx = [c * heads + h for h in range(heads)]
        states = [s_ref[h] for h in range(heads)]
        wq = [jnp.concatenate([sols[i][:, hd:], qs[i] * gt["eg"][:, h:h + 1]], axis=0).astype(BF16)
              for h, i in enumerate(idx)]
        kd_t = [(ks[i] * gt["ekd"][:, h:h + 1]).T.astype(BF16) for h, i in enumerate(idx)]
        wq_s = [dot(wq[h], states[h].astype(BF16)) for h in range(heads)]
        v_new = [(sols[i][:, :hd] - wq_s[h][:C]).astype(BF16) for h, i in enumerate(idx)]
        for h in range(heads):
            s_ref[h] = states[h] * gt["eg_last"][:, h:h + 1] + dot(kd_t[h], v_new[h])
        for h, i in enumerate(idx):
            o = wq_s[h][C:] + dot(attns[i], v_new[h])
            z = g_ref[rows, (3 * heads + h) * hd:(3 * heads + h + 1) * hd]
            y_ref[rows, h * hd:(h + 1) * hd] = _rms(o, wnorm) * _silu(z)


def _gdn(og, oab, wconv, alog, dtb, wnorm, batch, seq, lt):
    T, gw = og.shape
    width = gw // 4
    heads = GDN_HEADS
    nt = seq // lt
    body = functools.partial(_gdn_body, lt=lt, width=width, heads=heads)
    step = lt // SUBLANES
    return pl.pallas_call(
        body,
        grid=(batch, nt),
        in_specs=[pl.BlockSpec((lt, gw), lambda b, t: (b * nt + t, 0)),
                  pl.BlockSpec((SUBLANES, 3 * width), lambda b, t: (jnp.maximum((b * nt + t) * step - 1, 0), 0)),
                  pl.BlockSpec((lt, LANES), lambda b, t: (b * nt + t, 0)),
                  pl.BlockSpec(wconv.shape, lambda b, t: (0, 0)),
                  pl.BlockSpec(alog.shape, lambda b, t: (0, 0)),
                  pl.BlockSpec(dtb.shape, lambda b, t: (0, 0)),
                  pl.BlockSpec(wnorm.shape, lambda b, t: (0, 0))],
        out_specs=pl.BlockSpec((lt, width), lambda b, t: (b * nt + t, 0)),
        out_shape=jax.ShapeDtypeStruct((T, width), F32),
        scratch_shapes=[pltpu.VMEM((heads, width // heads, width // heads), F32),
                        pltpu.VMEM((lt, 3 * width), F32)],
        compiler_params=pltpu.CompilerParams(dimension_semantics=("arbitrary", "arbitrary"),
                                             vmem_limit_bytes=VMEM_LIMIT),
        name="gdn",
    )(og, og, oab, wconv, alog, dtb, wnorm)


def _sb_body(q_ref, k_ref, v_ref, o_ref, *, nsub, hd):
    bq = SB_BLOCK
    step = pl.program_id(2)
    scale = hd ** -0.5
    lane = lax.broadcasted_iota(jnp.int32, (1, LANES), 1)
    first_head = lane < hd

    def later(n):
        return (lax.broadcasted_iota(jnp.int32, (n, n), 0) > lax.broadcasted_iota(jnp.int32, (n, n), 1)).astype(BF16)

    def keys_after(lom, later_m):
        hi = lom.astype(BF16)
        lo = (lom - hi.astype(F32)).astype(BF16)
        return jnp.dot(hi, later_m, preferred_element_type=F32) + jnp.dot(lo, later_m, preferred_element_type=F32)

    def stack_heads(q):
        zero = jnp.zeros_like(q)
        return jnp.concatenate([jnp.where(first_head, q, zero), jnp.where(first_head, zero, q)], axis=0)

    row = lax.broadcasted_iota(jnp.int32, (2 * bq, 2 * bq), 0)
    qpos = jnp.where(row >= bq, row - bq, row)
    col = lax.broadcasted_iota(jnp.int32, (2 * bq, 2 * bq), 1)
    later2 = later(2 * bq)

    blocks = [step * nsub + s for s in range(nsub)]
    starts = [jnp.maximum(i - 1, 0) * bq for i in blocks]
    qqs = [stack_heads(q_ref[s * bq:(s + 1) * bq, :]) for s in range(nsub)]
    wins = [pl.ds(pl.multiple_of(w, bq), 2 * bq) for w in starts]
    zs = [lax.dot_general(qq, k_ref[win, :], _NT, preferred_element_type=F32) * scale for qq, win in zip(qqs, wins)]
    stricts = [col < qpos + (i * bq - w) for i, w in zip(blocks, starts)]
    loms = [jnp.where(st, -_softplus(z), 0.0) for st, z in zip(stricts, zs)]
    tails = [keys_after(lom, later2) for lom in loms]
    weights = [jnp.where(st, jnp.exp(z + lom + tail), 0.0).astype(BF16)
               for st, z, lom, tail in zip(stricts, zs, loms, tails)]
    accs = [jnp.dot(a, v_ref[win, :], preferred_element_type=F32) for a, win in zip(weights, wins)]
    carries = [jnp.sum(lom, axis=-1, keepdims=True) for lom in loms]

    later1 = later(bq)

    def cond(state):
        j, carry, _ = state
        return jnp.logical_and(j >= 0, jnp.max(carry) > SB_SKIP_LOG)

    for s in range(nsub):
        qq = qqs[s]

        def body(state, qq=qq):
            j, carry, acc = state
            rows = pl.ds(pl.multiple_of(j * bq, bq), bq)
            z = lax.dot_general(qq, k_ref[rows, :], _NT, preferred_element_type=F32) * scale
            lom = -_softplus(z)
            a = jnp.exp(z + lom + keys_after(lom, later1) + carry)
            acc = acc + jnp.dot(a.astype(BF16), v_ref[rows, :], preferred_element_type=F32)
            return j - 1, carry + jnp.sum(lom, axis=-1, keepdims=True), acc

        _, _, acc = lax.while_loop(cond, body, (blocks[s] - 2, carries[s], accs[s]))
        o_ref[s * bq:(s + 1) * bq, :] = jnp.where(first_head, acc[:bq], acc[bq:])


def _sb(osb, batch, seq, nsub):
    T, w3 = osb.shape
    width = w3 // 3
    pairs = width // LANES
    hd = width // SB_HEADS
    tq = nsub * SB_BLOCK
    nq = seq // tq
    assert seq % tq == 0 and seq >= 2 * SB_BLOCK
    body = functools.partial(_sb_body, nsub=nsub, hd=hd)
    return pl.pallas_call(
        body,
        grid=(batch, pairs, nq),
        in_specs=[pl.BlockSpec((tq, LANES), lambda b, p, i: (b * nq + i, p)),
                  pl.BlockSpec((seq, LANES), lambda b, p, i: (b, pairs + p)),
                  pl.BlockSpec((seq, LANES), lambda b, p, i: (b, 2 * pairs + p))],
        out_specs=pl.BlockSpec((tq, LANES), lambda b, p, i: (b * nq + i, p)),
        out_shape=jax.ShapeDtypeStruct((T, width), F32),
        compiler_params=pltpu.CompilerParams(dimension_semantics=("arbitrary", "arbitrary", "arbitrary"),
                                             vmem_limit_bytes=VMEM_LIMIT),
        name="stickbreak",
    )(osb, osb, osb)


def _mixout_body(x_ref, sc_ref, scprev_ref, yg_ref, ysb_ref, wsconv_ref, wa_ref, wb_ref, wc_ref, o_ref,
                 *, tm, tiles_per_seq, scw):
    i = pl.program_id(0)
    keep = jnp.where(i % tiles_per_seq > 0, 1.0, 0.0).astype(F32)
    p = sc_ref[:, scw:2 * scw] * sc_ref[:, 2 * scw:3 * scw]
    pprev = scprev_ref[:, scw:2 * scw] * scprev_ref[:, 2 * scw:3 * scw] * keep
    pp = jnp.concatenate([pprev, p], axis=0)
    w = wsconv_ref[...]
    conv = pp[SUBLANES:] * w[SC_KERNEL - 1:SC_KERNEL]
    for k in range(1, SC_KERNEL):
        conv = conv + pp[SUBLANES - k:SUBLANES - k + tm] * w[SC_KERNEL - 1 - k:SC_KERNEL - k]
    ysc = sc_ref[:, 0:scw] * conv
    acc = jnp.dot(ysc.astype(BF16), wa_ref[...], preferred_element_type=F32)
    acc = acc + jnp.dot(yg_ref[...].astype(BF16), wb_ref[...], preferred_element_type=F32)
    acc = acc + jnp.dot(ysb_ref[...].astype(BF16), wc_ref[...], preferred_element_type=F32)
    o_ref[...] = x_ref[...] + acc


def _mixout(x2, osc, yg, ysb, wsconv, wa, wb, wc, seq, tm):
    T, D = x2.shape
    scw = wa.shape[0]
    body = functools.partial(_mixout_body, tm=tm, tiles_per_seq=seq // tm, scw=scw)
    row = lambda n: pl.BlockSpec((tm, n), lambda i: (i, 0))
    return pl.pallas_call(
        body,
        grid=(T // tm,),
        in_specs=[row(D), row(osc.shape[1]), pl.BlockSpec((SUBLANES, osc.shape[1]), _prev_rows_map(tm)),
                  row(yg.shape[1]), row(ysb.shape[1]), _const_spec(wsconv.shape),
                  _const_spec(wa.shape), _const_spec(wb.shape), _const_spec(wc.shape)],
        out_specs=row(D),
        out_shape=jax.ShapeDtypeStruct((T, D), F32),
        compiler_params=pltpu.CompilerParams(dimension_semantics=("arbitrary",), vmem_limit_bytes=VMEM_LIMIT),
        name="mixout",
    )(x2, osc, osc, yg, ysb, wsconv, wa, wb, wc)


def _ffn_body(x_ref, xprev_ref, wn_ref, wup_ref, wconv_ref, wdown_ref, wfin_ref, o_ref, act_ref,
              *, tm, tiles_per_seq, dff, cw, final_norm):
    i = pl.program_id(0)
    keep = jnp.where(i % tiles_per_seq > 0, 1.0, 0.0).astype(F32)
    x = x_ref[...]
    xp = jnp.concatenate([xprev_ref[...] * keep, x], axis=0)
    h = _rms(xp, wn_ref[...]).astype(BF16)

    def conv(u, cols):
        w = wconv_ref[:, cols]
        out = u[SUBLANES:] * w[FFN_CONV - 1:FFN_CONV]
        for k in range(1, FFN_CONV):
            out = out + u[SUBLANES - k:SUBLANES - k + tm] * w[FFN_CONV - 1 - k:FFN_CONV - k]
        return out

    for j in range(dff // cw):
        gcols = slice(j * cw, (j + 1) * cw)
        vcols = slice(dff + j * cw, dff + (j + 1) * cw)
        gate = conv(jnp.dot(h, wup_ref[:, gcols], preferred_element_type=F32), gcols)
        val = conv(jnp.dot(h, wup_ref[:, vcols], preferred_element_type=F32), vcols)
        act_ref[:, gcols] = (_silu(gate) * val).astype(BF16)
    y = x + jnp.dot(act_ref[...], wdown_ref[...], preferred_element_type=F32)
    if final_norm:
        y = _rms(y, wfin_ref[...])
    o_ref[...] = y


def _ffn(x2, wn, wup, wconv, wdown, wfin, seq, tm, final_norm):
    T, D = x2.shape
    dff = wdown.shape[0]
    body = functools.partial(_ffn_body, tm=tm, tiles_per_seq=seq // tm, dff=dff, cw=256, final_norm=final_norm)
    return pl.pallas_call(
        body,
        grid=(T // tm,),
        in_specs=[pl.BlockSpec((tm, D), lambda i: (i, 0)), pl.BlockSpec((SUBLANES, D), _prev_rows_map(tm)),
                  _const_spec(wn.shape), _const_spec(wup.shape), _const_spec(wconv.shape),
                  _const_spec(wdown.shape), _const_spec(wfin.shape)],
        out_specs=pl.BlockSpec((tm, D), lambda i: (i, 0)),
        out_shape=jax.ShapeDtypeStruct((T, D), F32),
        scratch_shapes=[pltpu.VMEM((tm, dff), BF16)],
        compiler_params=pltpu.CompilerParams(dimension_semantics=("arbitrary",), vmem_limit_bytes=VMEM_LIMIT),
        name="ffn",
    )(x2, x2, wn, wup, wconv, wdown, wfin)


def _pad_lanes(a):
    return jnp.pad(a, ((0, 0), (0, LANES - a.shape[1])))


def kernel(x, w_norm_mix, w_mix_in, w_sconv, w_gdn_conv, gdn_a_log, gdn_dt_bias, w_gdn_norm, w_mix_out,
           w_norm_ffn, w_ffn_up, w_ffn_conv, w_ffn_down, w_norm_final):
    batch, seq, D = x.shape
    depth = w_mix_in.shape[0]
    scw = D // 4
    gw = D // 2
    sbw = D - scw - gw
    c_g = 3 * scw
    c_ab = c_g + 4 * gw
    c_sb = c_ab + 2 * GDN_HEADS
    assert w_mix_in.shape[2] == c_sb + 3 * sbw and gw // GDN_HEADS == LANES and sbw % LANES == 0

    x2 = x.reshape(batch * seq, D).astype(F32)
    row = lambda v: v.reshape(1, -1).astype(F32)
    for l in range(depth):
        w_in = w_mix_in[l]
        wsc = w_in[:, :c_g].astype(BF16)
        wg = w_in[:, c_g:c_ab].astype(BF16)
        wab = _pad_lanes(w_in[:, c_ab:c_sb]).astype(BF16)
        wsb = w_in[:, c_sb:].astype(BF16)
        osc, og, osb, oab = _inproj(x2, row(w_norm_mix[l]), wsc, wg, wsb, wab, tm=512)

        yg = _gdn(og, oab, w_gdn_conv[l].astype(F32), _pad_lanes(row(gdn_a_log[l])), _pad_lanes(row(gdn_dt_bias[l])),
                  row(w_gdn_norm[l]), batch, seq, lt=256)
        ysb = _sb(osb, batch, seq, nsub=4)

        w_out = w_mix_out[l].astype(BF16)
        x2 = _mixout(x2, osc, yg, ysb, w_sconv[l].astype(F32), w_out[:scw], w_out[scw:scw + gw], w_out[scw + gw:],
                     seq, tm=512)
        x2 = _ffn(x2, row(w_norm_ffn[l]), w_ffn_up[l].astype(BF16), w_ffn_conv[l].astype(F32),
                  w_ffn_down[l].astype(BF16), row(w_norm_final), seq, tm=512, final_norm=(l == depth - 1))
    return x2.reshape(batch, seq, D).astype(x.dtype)
```

```python
import functools

import jax
import jax.numpy as jnp
from jax import lax
from jax.experimental import pallas as pl
from jax.experimental.pallas import tpu as pltpu

F32 = jnp.float32
BF16 = jnp.bfloat16
NORM_EPS = 1e-6

LANES = 128
SUBLANES = 8
VMEM_LIMIT = 56 * 1024 * 1024

SC_KERNEL = 3
GDN_HEADS = 4
GDN_CONV = 4
GDN_CHUNK = 64
INV_BASE = 8
SB_HEADS = 4
SB_BLOCK = 128
FFN_CONV = 3
FFN_COLS = 256
SB_SKIP_LOG = -88.0

_NT = (((1,), (1,)), ((), ()))


def _rms(x, w):
    return x * lax.rsqrt(jnp.mean(x * x, axis=-1, keepdims=True) + NORM_EPS) * w


def _softplus(x):
    return jnp.maximum(x, 0.0) + jnp.log1p(jnp.exp(-jnp.abs(x)))


def _sigmoid(x):
    return 1.0 / (1.0 + jnp.exp(-x))


def _silu(x):
    return x * _sigmoid(x)


def _const_spec(shape):
    nd = len(shape)
    return pl.BlockSpec(shape, lambda *_: (0,) * nd, pipeline_mode=pl.Buffered(1))


def _prev_rows_map(rows_per_tile):
    step = rows_per_tile // SUBLANES
    return lambda i: (jnp.maximum(i * step - 1, 0), 0)


def _inproj_body(x_ref, wn_ref, wsc_ref, wg_ref, wsb_ref, wab_ref, wconv_ref, osc_ref, og_ref, osb_ref, oab_ref,
                 halo_ref, *, tm, tiles_per_seq, heads):
    i = pl.program_id(0)

    @pl.when(i == 0)
    def _():
        halo_ref[...] = jnp.zeros_like(halo_ref)

    h = _rms(x_ref[...], wn_ref[...]).astype(BF16)
    keep = i % tiles_per_seq > 0
    hd = LANES
    wconv = wconv_ref[...]
    for s in range(3 * heads):
        cols = slice(s * hd, (s + 1) * hd)
        if s % 2 == 0:
            pair = jnp.dot(h, wg_ref[:, s * hd:(s + 2) * hd], preferred_element_type=F32)
        raw = pair[:, (s % 2) * hd:(s % 2 + 1) * hd]
        xp = jnp.concatenate([jnp.where(keep, halo_ref[:, cols], 0.0), raw], axis=0)
        halo_ref[:, cols] = raw[tm - SUBLANES:]
        conv = xp[SUBLANES:] * wconv[GDN_CONV - 1:GDN_CONV, cols]
        for k in range(1, GDN_CONV):
            conv = conv + xp[SUBLANES - k:SUBLANES - k + tm] * wconv[GDN_CONV - 1 - k:GDN_CONV - k, cols]
        a = _silu(conv)
        if s < 2 * heads:
            a = a * lax.rsqrt(jnp.sum(a * a, axis=-1, keepdims=True) + NORM_EPS)
        if s < heads:
            a = a * (hd ** -0.5)
        og_ref[:, cols] = a
    zcols = slice(3 * heads * hd, 4 * heads * hd)
    og_ref[:, zcols] = jnp.dot(h, wg_ref[:, zcols], preferred_element_type=F32)
    osc_ref[...] = jnp.dot(h, wsc_ref[...], preferred_element_type=F32)
    osb_ref[...] = jnp.dot(h, wsb_ref[...], preferred_element_type=F32).astype(BF16)
    oab_ref[...] = jnp.dot(h, wab_ref[...], preferred_element_type=F32)


def _inproj(x2, wn, wsc, wg, wsb, wab, wconv, seq, tm):
    T, D = x2.shape
    outs = (wsc.shape[1], wg.shape[1], wsb.shape[1], wab.shape[1])
    body = functools.partial(_inproj_body, tm=tm, tiles_per_seq=seq // tm, heads=GDN_HEADS)
    return pl.pallas_call(
        body,
        grid=(T // tm,),
        in_specs=[pl.BlockSpec((tm, D), lambda i: (i, 0)), _const_spec(wn.shape), _const_spec(wsc.shape),
                  _const_spec(wg.shape), _const_spec(wsb.shape), _const_spec(wab.shape), _const_spec(wconv.shape)],
        out_specs=[pl.BlockSpec((tm, n), lambda i: (i, 0)) for n in outs],
        out_shape=[jax.ShapeDtypeStruct((T, outs[0]), F32), jax.ShapeDtypeStruct((T, outs[1]), F32),
                   jax.ShapeDtypeStruct((T, outs[2]), BF16), jax.ShapeDtypeStruct((T, outs[3]), F32)],
        scratch_shapes=[pltpu.VMEM((SUBLANES, wconv.shape[1]), F32)],
        compiler_params=pltpu.CompilerParams(dimension_semantics=("arbitrary",), vmem_limit_bytes=VMEM_LIMIT),
        name="inproj",
    )(x2, wn, wsc, wg, wsb, wab, wconv)


def _split2(x):
    hi = x.astype(BF16)
    lo = (x - hi.astype(F32)).astype(BF16)
    return hi, lo


def _dot_split(a, b):
    ah, al = a
    bh, bl = b
    dot = functools.partial(jnp.dot, preferred_element_type=F32)
    return dot(ah, bh) + (dot(ah, bl) + dot(al, bh))


def _gdn_body(g_ref, ab_ref, alog_ref, dtb_ref, wnorm_ref, y_ref, s_ref, *, lt, width, heads):
    hd = width // heads
    C = GDN_CHUNK
    nc = lt // C
    t = pl.program_id(1)

    @pl.when(t == 0)
    def _():
        s_ref[...] = jnp.zeros_like(s_ref)

    ri = lax.broadcasted_iota(jnp.int32, (C, C), 0)
    ci = lax.broadcasted_iota(jnp.int32, (C, C), 1)
    causal = ri >= ci
    strict = ri > ci
    eye = (ri == ci).astype(F32)
    ltri = causal.astype(BF16)
    neg_a = -jnp.exp(alog_ref[...])
    dtb = dtb_ref[...]
    wnorm = wnorm_ref[...]
    dot = functools.partial(jnp.dot, preferred_element_type=F32)

    tiles = [(c, h) for c in range(nc) for h in range(heads)]
    gate = []
    for c in range(nc):
        rows = slice(c * C, (c + 1) * C)
        ab = ab_ref[rows, :]
        g_all = neg_a * _softplus(ab + dtb)
        g1 = g_all.astype(BF16)
        r1 = g_all - g1.astype(F32)
        g2 = r1.astype(BF16)
        g3 = (r1 - g2.astype(F32)).astype(BF16)
        gc_all = dot(ltri, g1) + (dot(ltri, g2) + dot(ltri, g3))
        g_last = gc_all[C - 1:C, :]
        gate.append(dict(beta=_sigmoid(ab), gc=gc_all, gc_t=gc_all.T, eg=jnp.exp(gc_all),
                         ekd=jnp.exp(g_last - gc_all), eg_last=jnp.exp(g_last)))

    ks, qs, decays, kbs, rhss = [], [], [], [], []
    for c, h in tiles:
        gt = gate[c]
        rows = slice(c * C, (c + 1) * C)
        q = g_ref[rows, h * hd:(h + 1) * hd]
        k = g_ref[rows, (heads + h) * hd:(heads + h + 1) * hd]
        v = g_ref[rows, (2 * heads + h) * hd:(2 * heads + h + 1) * hd]
        beta = gt["beta"][:, heads + h:heads + h + 1]
        diff = gt["gc"][:, h:h + 1] - gt["gc_t"][h:h + 1, :]
        decays.append(jnp.where(causal, jnp.exp(jnp.where(causal, diff, 0.0)), 0.0))
        kb = k * beta
        ks.append(k)
        qs.append(q)
        kbs.append(kb)
        rhss.append(jnp.concatenate([v * beta, kb * gt["eg"][:, h:h + 1]], axis=1))
    kk = [lax.dot_general(kb.astype(BF16), k.astype(BF16), _NT, preferred_element_type=F32) for kb, k in zip(kbs, ks)]
    qk = [lax.dot_general(q.astype(BF16), k.astype(BF16), _NT, preferred_element_type=F32) for q, k in zip(qs, ks)]
    lms = [jnp.where(strict, x * d, 0.0) for x, d in zip(kk, decays)]
    attns = [(x * d).astype(BF16) for x, d in zip(qk, decays)]

    def same_block(n):
        shift = n.bit_length() - 1
        return jnp.right_shift(ri, shift) == jnp.right_shift(ci, shift)

    inner = same_block(INV_BASE)
    diag = [jnp.where(inner, lm, 0.0) for lm in lms]
    ms = [_split2(d) for d in diag]
    ps = [eye - d for d in diag]
    span = 2
    while span < INV_BASE:
        ms = [_split2(_dot_split(m, m)) for m in ms]
        ps = [p + _dot_split(_split2(p), m) for p, m in zip(ps, ms)]
        span *= 2
    size = INV_BASE
    while size < C:
        outer = same_block(2 * size)
        below = [_split2(jnp.where(jnp.logical_and(outer, jnp.logical_not(inner)), lm, 0.0)) for lm in lms]
        tsplit = [_split2(p) for p in ps]
        bt = [_split2(_dot_split(b_, t_)) for b_, t_ in zip(below, tsplit)]
        ps = [p - _dot_split(t_, x) for p, t_, x in zip(ps, tsplit, bt)]
        inner = outer
        size *= 2
    sols = [_dot_split(_split2(p), _split2(r)) for p, r in zip(ps, rhss)]

    for c in range(nc):
        gt = gate[c]
        rows = slice(c * C, (c + 1) * C)
        idx = [c * heads + h for h in range(heads)]
        states = [s_ref[h] for h in range(heads)]
        wq = [jnp.concatenate([sols[i][:, hd:], qs[i] * gt["eg"][:, h:h + 1]], axis=0).astype(BF16)
              for h, i in enumerate(idx)]
        kd_t = [(ks[i] * gt["ekd"][:, h:h + 1]).T.astype(BF16) for h, i in enumerate(idx)]
        wq_s = [dot(wq[h], states[h].astype(BF16)) for h in range(heads)]
        v_new = [(sols[i][:, :hd] - wq_s[h][:C]).astype(BF16) for h, i in enumerate(idx)]
        for h in range(heads):
            s_ref[h] = states[h] * gt["eg_last"][:, h:h + 1] + dot(kd_t[h], v_new[h])
        for h, i in enumerate(idx):
            o = wq_s[h][C:] + dot(attns[i], v_new[h])
            z = g_ref[rows, (3 * heads + h) * hd:(3 * heads + h + 1) * hd]
            y_ref[rows, h * hd:(h + 1) * hd] = (_rms(o, wnorm) * _silu(z)).astype(y_ref.dtype)


def _gdn(og, oab, alog, dtb, wnorm, batch, seq, lt):
    T, gw = og.shape
    width = gw // 4
    heads = GDN_HEADS
    nt = seq // lt
    body = functools.partial(_gdn_body, lt=lt, width=width, heads=heads)
    return pl.pallas_call(
        body,
        grid=(batch, nt),
        in_specs=[pl.BlockSpec((lt, gw), lambda b, t: (b * nt + t, 0)),
                  pl.BlockSpec((lt, LANES), lambda b, t: (b * nt + t, 0)),
                  pl.BlockSpec(alog.shape, lambda b, t: (0, 0)),
                  pl.BlockSpec(dtb.shape, lambda b, t: (0, 0)),
                  pl.BlockSpec(wnorm.shape, lambda b, t: (0, 0))],
        out_specs=pl.BlockSpec((lt, width), lambda b, t: (b * nt + t, 0)),
        out_shape=jax.ShapeDtypeStruct((T, width), BF16),
        scratch_shapes=[pltpu.VMEM((heads, width // heads, width // heads), F32)],
        compiler_params=pltpu.CompilerParams(dimension_semantics=("arbitrary", "arbitrary"),
                                             vmem_limit_bytes=VMEM_LIMIT),
        name="gdn",
    )(og, oab, alog, dtb, wnorm)


def _sb_body(q_ref, k_ref, v_ref, o_ref, *, nsub, hd):
    bq = SB_BLOCK
    step = pl.program_id(2)
    scale = hd ** -0.5
    lane = lax.broadcasted_iota(jnp.int32, (1, LANES), 1)
    first_head = lane < hd

    def later(n):
        return (lax.broadcasted_iota(jnp.int32, (n, n), 0) > lax.broadcasted_iota(jnp.int32, (n, n), 1)).astype(BF16)

    def keys_after(lom, later_m):
        hi = lom.astype(BF16)
        lo = (lom - hi.astype(F32)).astype(BF16)
        return jnp.dot(hi, later_m, preferred_element_type=F32) + jnp.dot(lo, later_m, preferred_element_type=F32)

    def stack_heads(q):
        zero = jnp.zeros_like(q)
        return jnp.concatenate([jnp.where(first_head, q, zero), jnp.where(first_head, zero, q)], axis=0)

    row = lax.broadcasted_iota(jnp.int32, (2 * bq, 2 * bq), 0)
    qpos = jnp.where(row >= bq, row - bq, row)
    col = lax.broadcasted_iota(jnp.int32, (2 * bq, 2 * bq), 1)
    later2 = later(2 * bq)

    blocks = [step * nsub + s for s in range(nsub)]
    starts = [jnp.maximum(i - 1, 0) * bq for i in blocks]
    qqs = [stack_heads(q_ref[s * bq:(s + 1) * bq, :]) for s in range(nsub)]
    wins = [pl.ds(pl.multiple_of(w, bq), 2 * bq) for w in starts]
    zs = [lax.dot_general(qq, k_ref[win, :], _NT, preferred_element_type=F32) * scale for qq, win in zip(qqs, wins)]
    stricts = [col < qpos + (i * bq - w) for i, w in zip(blocks, starts)]
    loms = [jnp.where(st, -_softplus(z), 0.0) for st, z in zip(stricts, zs)]
    tails = [keys_after(lom, later2) for lom in loms]
    weights = [jnp.where(st, jnp.exp(z + lom + tail), 0.0).astype(BF16)
               for st, z, lom, tail in zip(stricts, zs, loms, tails)]
    accs = [jnp.dot(a, v_ref[win, :], preferred_element_type=F32) for a, win in zip(weights, wins)]
    carries = [jnp.sum(lom, axis=-1, keepdims=True) for lom in loms]

    later1 = later(bq)

    def cond(state):
        j, carry, _ = state
        return jnp.logical_and(j >= 0, jnp.max(carry) > SB_SKIP_LOG)

    for s in range(nsub):
        qq = qqs[s]

        def body(state, qq=qq):
            j, carry, acc = state
            rows = pl.ds(pl.multiple_of(j * bq, bq), bq)
            z = lax.dot_general(qq, k_ref[rows, :], _NT, preferred_element_type=F32) * scale
            lom = -_softplus(z)
            a = jnp.exp(z + lom + keys_after(lom, later1) + carry)
            acc = acc + jnp.dot(a.astype(BF16), v_ref[rows, :], preferred_element_type=F32)
            return j - 1, carry + jnp.sum(lom, axis=-1, keepdims=True), acc

        _, _, acc = lax.while_loop(cond, body, (blocks[s] - 2, carries[s], accs[s]))
        o_ref[s * bq:(s + 1) * bq, :] = jnp.where(first_head, acc[:bq], acc[bq:]).astype(o_ref.dtype)


def _sb(osb, batch, seq, nsub):
    T, w3 = osb.shape
    width = w3 // 3
    pairs = width // LANES
    hd = width // SB_HEADS
    tq = nsub * SB_BLOCK
    nq = seq // tq
    assert seq % tq == 0 and seq >= 2 * SB_BLOCK
    body = functools.partial(_sb_body, nsub=nsub, hd=hd)
    return pl.pallas_call(
        body,
        grid=(batch, pairs, nq),
        in_specs=[pl.BlockSpec((tq, LANES), lambda b, p, i: (b * nq + i, p)),
                  pl.BlockSpec((seq, LANES), lambda b, p, i: (b, pairs + p)),
                  pl.BlockSpec((seq, LANES), lambda b, p, i: (b, 2 * pairs + p))],
        out_specs=pl.BlockSpec((tq, LANES), lambda b, p, i: (b * nq + i, p)),
        out_shape=jax.ShapeDtypeStruct((T, width), BF16),
        compiler_params=pltpu.CompilerParams(dimension_semantics=("arbitrary", "arbitrary", "arbitrary"),
                                             vmem_limit_bytes=VMEM_LIMIT),
        name="stickbreak",
    )(osb, osb, osb)


def _mixffn_body(x_ref, sc_ref, scprev_ref, yg_ref, ysb_ref, wsconv_ref, wa_ref, wb_ref, wc_ref,
                 wn_ref, wup_ref, wconv_ref, wdown_ref, wfin_ref, o_ref, act_ref, halo_ref,
                 *, tm, tiles_per_seq, scw, dff, cw, final_norm):
    i = pl.program_id(0)

    @pl.when(i == 0)
    def _():
        halo_ref[...] = jnp.zeros_like(halo_ref)

    keep = i % tiles_per_seq > 0

    def causal_conv(u, w, taps):
        out = u[SUBLANES:] * w[taps - 1:taps]
        for k in range(1, taps):
            out = out + u[SUBLANES - k:SUBLANES - k + tm] * w[taps - 1 - k:taps - k]
        return out

    p = sc_ref[:, scw:2 * scw] * sc_ref[:, 2 * scw:3 * scw]
    pprev = jnp.where(keep, scprev_ref[:, scw:2 * scw] * scprev_ref[:, 2 * scw:3 * scw], 0.0)
    ysc = sc_ref[:, 0:scw] * causal_conv(jnp.concatenate([pprev, p], axis=0), wsconv_ref[...], SC_KERNEL)
    xn = x_ref[...] + jnp.dot(ysc.astype(BF16), wa_ref[...], preferred_element_type=F32)
    xn = xn + jnp.dot(yg_ref[...], wb_ref[...], preferred_element_type=F32)
    xn = xn + jnp.dot(ysb_ref[...], wc_ref[...], preferred_element_type=F32)

    xp = jnp.concatenate([jnp.where(keep, halo_ref[...], 0.0), xn], axis=0)
    halo_ref[...] = xn[tm - SUBLANES:]
    h = _rms(xp, wn_ref[...]).astype(BF16)
    for j in range(dff // cw):
        gcols = slice(j * cw, (j + 1) * cw)
        vcols = slice(dff + j * cw, dff + (j + 1) * cw)
        gate = causal_conv(jnp.dot(h, wup_ref[:, gcols], preferred_element_type=F32), wconv_ref[:, gcols], FFN_CONV)
        val = causal_conv(jnp.dot(h, wup_ref[:, vcols], preferred_element_type=F32), wconv_ref[:, vcols], FFN_CONV)
        act_ref[:, gcols] = (_silu(gate) * val).astype(BF16)
    y = xn + jnp.dot(act_ref[...], wdown_ref[...], preferred_element_type=F32)
    if final_norm:
        y = _rms(y, wfin_ref[...])
    o_ref[...] = y


def _mixffn(x2, osc, yg, ysb, wsconv, wa, wb, wc, wn, wup, wconv, wdown, wfin, seq, tm, final_norm):
    T, D = x2.shape
    scw = wa.shape[0]
    dff = wdown.shape[0]
    body = functools.partial(_mixffn_body, tm=tm, tiles_per_seq=seq // tm, scw=scw, dff=dff, cw=FFN_COLS,
                             final_norm=final_norm)
    row = lambda n: pl.BlockSpec((tm, n), lambda i: (i, 0))
    consts = (wsconv, wa, wb, wc, wn, wup, wconv, wdown, wfin)
    return pl.pallas_call(
        body,
        grid=(T // tm,),
        in_specs=[row(D), row(osc.shape[1]), pl.BlockSpec((SUBLANES, osc.shape[1]), _prev_rows_map(tm)),
                  row(yg.shape[1]), row(ysb.shape[1])] + [_const_spec(c.shape) for c in consts],
        out_specs=row(D),
        out_shape=jax.ShapeDtypeStruct((T, D), F32),
        scratch_shapes=[pltpu.VMEM((tm, dff), BF16), pltpu.VMEM((SUBLANES, D), F32)],
        compiler_params=pltpu.CompilerParams(dimension_semantics=("arbitrary",), vmem_limit_bytes=VMEM_LIMIT),
        name="mixffn",
    )(x2, osc, osc, yg, ysb, *consts)


def _pad_lanes(a):
    return jnp.pad(a, ((0, 0), (0, LANES - a.shape[1])))


def kernel(x, w_norm_mix, w_mix_in, w_sconv, w_gdn_conv, gdn_a_log, gdn_dt_bias, w_gdn_norm, w_mix_out,
           w_norm_ffn, w_ffn_up, w_ffn_conv, w_ffn_down, w_norm_final):
    batch, seq, D = x.shape
    depth = w_mix_in.shape[0]
    scw = D // 4
    gw = D // 2
    sbw = D - scw - gw
    c_g = 3 * scw
    c_ab = c_g + 4 * gw
    c_sb = c_ab + 2 * GDN_HEADS
    assert w_mix_in.shape[2] == c_sb + 3 * sbw and gw // GDN_HEADS == LANES and sbw % LANES == 0

    x2 = x.reshape(batch * seq, D).astype(F32)
    row = lambda v: v.reshape(1, -1).astype(F32)
    for l in range(depth):
        w_in = w_mix_in[l]
        wsc = w_in[:, :c_g].astype(BF16)
        wg = w_in[:, c_g:c_ab].astype(BF16)
        wab = _pad_lanes(w_in[:, c_ab:c_sb]).astype(BF16)
        wsb = w_in[:, c_sb:].astype(BF16)
        osc, og, osb, oab = _inproj(x2, row(w_norm_mix[l]), wsc, wg, wsb, wab, w_gdn_conv[l].astype(F32), seq, tm=512)

        yg = _gdn(og, oab, _pad_lanes(row(gdn_a_log[l])), _pad_lanes(row(gdn_dt_bias[l])),
                  row(w_gdn_norm[l]), batch, seq, lt=256)
        ysb = _sb(osb, batch, seq, nsub=4)

        w_out = w_mix_out[l].astype(BF16)
        x2 = _mixffn(x2, osc, yg, ysb, w_sconv[l].astype(F32), w_out[:scw], w_out[scw:scw + gw], w_out[scw + gw:],
                     row(w_norm_ffn[l]), w_ffn_up[l].astype(BF16), w_ffn_conv[l].astype(F32),
                     w_ffn_down[l].astype(BF16), row(w_norm_final), seq, tm=512, final_norm=(l == depth - 1))
    return x2.reshape(batch, seq, D).astype(x.dtype)
```

```python
import functools
import math

import jax
import jax.numpy as jnp
from jax import lax
from jax.experimental import pallas as pl
from jax.experimental.pallas import tpu as pltpu

F32 = jnp.float32
BF16 = jnp.bfloat16
NORM_EPS = 1e-6

LANES = 128
SUBLANES = 8
VMEM_LIMIT = 56 * 1024 * 1024

SC_KERNEL = 3
GDN_HEADS = 4
GDN_CONV = 4
GDN_CHUNK = 64
CONV_ROWS = 64
INV_BASE = 8
SB_HEADS = 4
SB_BLOCK = 128
FFN_CONV = 3
FFN_COLS = 256
SB_SKIP_LOG = -88.0

_NT = (((1,), (1,)), ((), ()))


def _rms(x, w):
    return x * lax.rsqrt(jnp.mean(x * x, axis=-1, keepdims=True) + NORM_EPS) * w


def _softplus(x):
    return jnp.maximum(x, 0.0) + jnp.log1p(jnp.exp(-jnp.abs(x)))


def _softplus_pos(x):
    return jnp.maximum(x, 0.0) + jnp.log(1.0 + jnp.exp(-jnp.abs(x)))


def _sigmoid(x):
    return 1.0 / (1.0 + jnp.exp(-x))


def _silu(x):
    return x * _sigmoid(x)


def _const_spec(shape):
    nd = len(shape)
    return pl.BlockSpec(shape, lambda *_: (0,) * nd, pipeline_mode=pl.Buffered(1))


def _prev_rows_map(rows_per_tile):
    step = rows_per_tile // SUBLANES
    return lambda i: (jnp.maximum(i * step - 1, 0), 0)


def _inproj_body(x_ref, wn_ref, wsc_ref, wg_ref, wsb_ref, wab_ref, wconv_ref, osc_ref, og_ref, osb_ref, oab_ref,
                 halo_ref, *, tm, tiles_per_seq, heads):
    i = pl.program_id(0)

    @pl.when(i == 0)
    def _():
        halo_ref[...] = jnp.zeros_like(halo_ref)

    h = _rms(x_ref[...], wn_ref[...]).astype(BF16)
    keep = i % tiles_per_seq > 0
    hd = LANES
    wconv = wconv_ref[...]
    def project(group):
        gcols = slice(group * heads * hd, (group + 1) * heads * hd)
        og_ref[:, gcols] = jnp.dot(h, wg_ref[:, gcols], preferred_element_type=F32)

    project(0)
    for s in range(3 * heads):
        cols = slice(s * hd, (s + 1) * hd)
        if s % heads == 0:
            project(s // heads + 1)
        above = jnp.where(keep, halo_ref[:, cols], 0.0)
        for r in range(0, tm, CONV_ROWS):
            raw = og_ref[r:r + CONV_ROWS, cols]
            xp = jnp.concatenate([above, raw], axis=0)
            above = raw[CONV_ROWS - SUBLANES:]
            conv = xp[SUBLANES:] * wconv[GDN_CONV - 1:GDN_CONV, cols]
            for k in range(1, GDN_CONV):
                conv = conv + xp[SUBLANES - k:SUBLANES - k + CONV_ROWS] * wconv[GDN_CONV - 1 - k:GDN_CONV - k, cols]
            a = _silu(conv)
            if s < 2 * heads:
                a = a * lax.rsqrt(jnp.sum(a * a, axis=-1, keepdims=True) + NORM_EPS)
            if s < heads:
                a = a * (hd ** -0.5)
            og_ref[r:r + CONV_ROWS, cols] = a
        halo_ref[:, cols] = above
    osc_ref[...] = jnp.dot(h, wsc_ref[...], preferred_element_type=F32)
    osb_ref[...] = jnp.dot(h, wsb_ref[...], preferred_element_type=F32).astype(BF16)
    oab_ref[...] = jnp.dot(h, wab_ref[...], preferred_element_type=F32)


def _inproj(x2, wn, wsc, wg, wsb, wab, wconv, seq, tm):
    T, D = x2.shape
    outs = (wsc.shape[1], wg.shape[1], wsb.shape[1], wab.shape[1])
    body = functools.partial(_inproj_body, tm=tm, tiles_per_seq=seq // tm, heads=GDN_HEADS)
    return pl.pallas_call(
        body,
        grid=(T // tm,),
        in_specs=[pl.BlockSpec((tm, D), lambda i: (i, 0)), _const_spec(wn.shape), _const_spec(wsc.shape),
                  _const_spec(wg.shape), _const_spec(wsb.shape), _const_spec(wab.shape), _const_spec(wconv.shape)],
        out_specs=[pl.BlockSpec((tm, n), lambda i: (i, 0)) for n in outs],
        out_shape=[jax.ShapeDtypeStruct((T, outs[0]), F32), jax.ShapeDtypeStruct((T, outs[1]), F32),
                   jax.ShapeDtypeStruct((T, outs[2]), BF16), jax.ShapeDtypeStruct((T, outs[3]), F32)],
        scratch_shapes=[pltpu.VMEM((SUBLANES, wconv.shape[1]), F32)],
        compiler_params=pltpu.CompilerParams(dimension_semantics=("arbitrary",), vmem_limit_bytes=VMEM_LIMIT),
        name="inproj",
    )(x2, wn, wsc, wg, wsb, wab, wconv)


def _split2(x):
    hi = x.astype(BF16)
    lo = (x - hi.astype(F32)).astype(BF16)
    return hi, lo


def _dot_split(a, b):
    ah, al = a
    bh, bl = b
    dot = functools.partial(jnp.dot, preferred_element_type=F32)
    return dot(ah, bh) + (dot(ah, bl) + dot(al, bh))


def _gdn_body(g_ref, ab_ref, alog_ref, dtb_ref, wnorm_ref, y_ref, s_ref, *, lt, width, heads):
    hd = width // heads
    C = GDN_CHUNK
    nc = lt // C
    t = pl.program_id(1)

    @pl.when(t == 0)
    def _():
        s_ref[...] = jnp.zeros_like(s_ref)

    ri = lax.broadcasted_iota(jnp.int32, (C, C), 0)
    ci = lax.broadcasted_iota(jnp.int32, (C, C), 1)
    causal = ri >= ci
    strict = ri > ci
    eye = (ri == ci).astype(F32)
    ltri = causal.astype(BF16)
    neg_a = -jnp.exp(alog_ref[...])
    dtb = dtb_ref[...]
    wnorm = wnorm_ref[...]
    dot = functools.partial(jnp.dot, preferred_element_type=F32)

    tiles = [(c, h) for c in range(nc) for h in range(heads)]
    gate = []
    for c in range(nc):
        rows = slice(c * C, (c + 1) * C)
        ab = ab_ref[rows, :]
        g_all = neg_a * _softplus(ab + dtb)
        g1 = g_all.astype(BF16)
        r1 = g_all - g1.astype(F32)
        g2 = r1.astype(BF16)
        g3 = (r1 - g2.astype(F32)).astype(BF16)
        gc_all = dot(ltri, g1) + (dot(ltri, g2) + dot(ltri, g3))
        g_last = gc_all[C - 1:C, :]
        gate.append(dict(beta=_sigmoid(ab), gc=gc_all, gc_t=gc_all.T, eg=jnp.exp(gc_all),
                         ekd=jnp.exp(g_last - gc_all), eg_last=jnp.exp(g_last)))

    ks, qs, decays, kbs, rhss = [], [], [], [], []
    for c, h in tiles:
        gt = gate[c]
        rows = slice(c * C, (c + 1) * C)
        q = g_ref[rows, h * hd:(h + 1) * hd]
        k = g_ref[rows, (heads + h) * hd:(heads + h + 1) * hd]
        v = g_ref[rows, (2 * heads + h) * hd:(2 * heads + h + 1) * hd]
        beta = gt["beta"][:, heads + h:heads + h + 1]
        diff = gt["gc"][:, h:h + 1] - gt["gc_t"][h:h + 1, :]
        decays.append(jnp.where(causal, jnp.exp(jnp.where(causal, diff, 0.0)), 0.0))
        kb = k * beta
        ks.append(k)
        qs.append(q)
        kbs.append(kb)
        rhss.append(jnp.concatenate([v * beta, kb * gt["eg"][:, h:h + 1]], axis=1))
    kk = [lax.dot_general(kb.astype(BF16), k.astype(BF16), _NT, preferred_element_type=F32) for kb, k in zip(kbs, ks)]
    qk = [lax.dot_general(q.astype(BF16), k.astype(BF16), _NT, preferred_element_type=F32) for q, k in zip(qs, ks)]
    lms = [jnp.where(strict, x * d, 0.0) for x, d in zip(kk, decays)]
    attns = [(x * d).astype(BF16) for x, d in zip(qk, decays)]

    def same_block(n):
        shift = n.bit_length() - 1
        return jnp.right_shift(ri, shift) == jnp.right_shift(ci, shift)

    inner = same_block(INV_BASE)
    diag = [jnp.where(inner, lm, 0.0) for lm in lms]
    ms = [_split2(d) for d in diag]
    ps = [eye - d for d in diag]
    span = 2
    while span < INV_BASE:
        ms = [_split2(_dot_split(m, m)) for m in ms]
        ps = [p + _dot_split(_split2(p), m) for p, m in zip(ps, ms)]
        span *= 2
    size = INV_BASE
    while size < C:
        outer = same_block(2 * size)
        below = [_split2(jnp.where(jnp.logical_and(outer, jnp.logical_not(inner)), lm, 0.0)) for lm in lms]
        tsplit = [_split2(p) for p in ps]
        bt = [_split2(_dot_split(b_, t_)) for b_, t_ in zip(below, tsplit)]
        ps = [p - _dot_split(t_, x) for p, t_, x in zip(ps, tsplit, bt)]
        inner = outer
        size *= 2
    sols = [_dot_split(_split2(p), _split2(r)) for p, r in zip(ps, rhss)]

    for c in range(nc):
        gt = gate[c]
        rows = slice(c * C, (c + 1) * C)
        idx = [c * heads + h for h in range(heads)]
        states = [s_ref[h] for h in range(heads)]
        wq = [jnp.concatenate([sols[i][:, hd:], qs[i] * gt["eg"][:, h:h + 1]], axis=0).astype(BF16)
              for h, i in enumerate(idx)]
        kd_t = [(ks[i] * gt["ekd"][:, h:h + 1]).T.astype(BF16) for h, i in enumerate(idx)]
        wq_s = [dot(wq[h], states[h].astype(BF16)) for h in range(heads)]
        v_new = [(sols[i][:, :hd] - wq_s[h][:C]).astype(BF16) for h, i in enumerate(idx)]
        for h in range(heads):
            s_ref[h] = states[h] * gt["eg_last"][:, h:h + 1] + dot(kd_t[h], v_new[h])
        for h, i in enumerate(idx):
            o = wq_s[h][C:] + dot(attns[i], v_new[h])
            z = g_ref[rows, (3 * heads + h) * hd:(3 * heads + h + 1) * hd]
            y_ref[rows, h * hd:(h + 1) * hd] = (_rms(o, wnorm) * _silu(z)).astype(y_ref.dtype)


def _gdn(og, oab, alog, dtb, wnorm, batch, seq, lt):
    T, gw = og.shape
    width = gw // 4
    heads = GDN_HEADS
    nt = seq // lt
    body = functools.partial(_gdn_body, lt=lt, width=width, heads=heads)
    return pl.pallas_call(
        body,
        grid=(batch, nt),
        in_specs=[pl.BlockSpec((lt, gw), lambda b, t: (b * nt + t, 0)),
                  pl.BlockSpec((lt, LANES), lambda b, t: (b * nt + t, 0)),
                  pl.BlockSpec(alog.shape, lambda b, t: (0, 0)),
                  pl.BlockSpec(dtb.shape, lambda b, t: (0, 0)),
                  pl.BlockSpec(wnorm.shape, lambda b, t: (0, 0))],
        out_specs=pl.BlockSpec((lt, width), lambda b, t: (b * nt + t, 0)),
        out_shape=jax.ShapeDtypeStruct((T, width), BF16),
        scratch_shapes=[pltpu.VMEM((heads, width // heads, width // heads), F32)],
        compiler_params=pltpu.CompilerParams(dimension_semantics=("arbitrary", "arbitrary"),
                                             vmem_limit_bytes=VMEM_LIMIT),
        name="gdn",
    )(og, oab, alog, dtb, wnorm)


def _sb_body(q_ref, k_ref, v_ref, o_ref, *, nsub, hd):
    bq = SB_BLOCK
    step = pl.program_id(2)
    scale = hd ** -0.5
    scale_on_q = math.frexp(scale)[0] == 0.5
    lane = lax.broadcasted_iota(jnp.int32, (1, LANES), 1)
    first_head = lane < hd

    def later(n):
        return (lax.broadcasted_iota(jnp.int32, (n, n), 0) > lax.broadcasted_iota(jnp.int32, (n, n), 1)).astype(BF16)

    def keys_after(lom, later_m):
        hi = lom.astype(BF16)
        lo = (lom - hi.astype(F32)).astype(BF16)
        return jnp.dot(hi, later_m, preferred_element_type=F32) + jnp.dot(lo, later_m, preferred_element_type=F32)

    def stack_heads(q):
        if scale_on_q:
            q = q * jnp.asarray(scale, q.dtype)
        zero = jnp.zeros_like(q)
        return jnp.concatenate([jnp.where(first_head, q, zero), jnp.where(first_head, zero, q)], axis=0)

    row = lax.broadcasted_iota(jnp.int32, (2 * bq, 2 * bq), 0)
    ahead = lax.broadcasted_iota(jnp.int32, (2 * bq, 2 * bq), 1) - jnp.where(row >= bq, row - bq, row)
    later2 = later(2 * bq)

    def logits(qq, rows):
        z = lax.dot_general(qq, k_ref[rows, :], _NT, preferred_element_type=F32)
        return z if scale_on_q else z * scale

    blocks = [step * nsub + s for s in range(nsub)]
    starts = [jnp.maximum(i - 1, 0) * bq for i in blocks]
    qqs = [stack_heads(q_ref[s * bq:(s + 1) * bq, :]) for s in range(nsub)]
    wins = [pl.ds(pl.multiple_of(w, bq), 2 * bq) for w in starts]
    zs = [logits(qq, win) for qq, win in zip(qqs, wins)]
    stricts = [ahead < i * bq - w for i, w in zip(blocks, starts)]
    loms = [jnp.where(st, -_softplus_pos(z), 0.0) for st, z in zip(stricts, zs)]
    tails = [keys_after(lom, later2) for lom in loms]
    weights = [jnp.where(st, jnp.exp(z + lom + tail), 0.0).astype(BF16)
               for st, z, lom, tail in zip(stricts, zs, loms, tails)]
    accs = [jnp.dot(a, v_ref[win, :], preferred_element_type=F32) for a, win in zip(weights, wins)]
    carries = [jnp.sum(lom, axis=-1, keepdims=True) for lom in loms]

    later1 = later(bq)

    def cond(state):
        j, carry, _ = state
        return jnp.logical_and(j >= 0, jnp.max(carry) > SB_SKIP_LOG)

    for s in range(nsub):
        qq = qqs[s]

        def body(state, qq=qq):
            j, carry, acc = state
            rows = pl.ds(pl.multiple_of(j * bq, bq), bq)
            z = logits(qq, rows)
            lom = -_softplus_pos(z)
            a = jnp.exp(z + lom + keys_after(lom, later1) + carry)
            acc = acc + jnp.dot(a.astype(BF16), v_ref[rows, :], preferred_element_type=F32)
            return j - 1, carry + jnp.sum(lom, axis=-1, keepdims=True), acc

        _, _, acc = lax.while_loop(cond, body, (blocks[s] - 2, carries[s], accs[s]))
        o_ref[s * bq:(s + 1) * bq, :] = jnp.where(first_head, acc[:bq], acc[bq:]).astype(o_ref.dtype)


def _sb(osb, batch, seq, nsub):
    T, w3 = osb.shape
    width = w3 // 3
    pairs = width // LANES
    hd = width // SB_HEADS
    tq = nsub * SB_BLOCK
    nq = seq // tq
    assert seq % tq == 0 and seq >= 2 * SB_BLOCK
    body = functools.partial(_sb_body, nsub=nsub, hd=hd)
    return pl.pallas_call(
        body,
        grid=(batch, pairs, nq),
        in_specs=[pl.BlockSpec((tq, LANES), lambda b, p, i: (b * nq + i, p)),
                  pl.BlockSpec((seq, LANES), lambda b, p, i: (b, pairs + p)),
                  pl.BlockSpec((seq, LANES), lambda b, p, i: (b, 2 * pairs + p))],
        out_specs=pl.BlockSpec((tq, LANES), lambda b, p, i: (b * nq + i, p)),
        out_shape=jax.ShapeDtypeStruct((T, width), BF16),
        compiler_params=pltpu.CompilerParams(dimension_semantics=("arbitrary", "arbitrary", "arbitrary"),
                                             vmem_limit_bytes=VMEM_LIMIT),
        name="stickbreak",
    )(osb, osb, osb)


def _mixffn_body(x_ref, sc_ref, scprev_ref, yg_ref, ysb_ref, wsconv_ref, wa_ref, wb_ref, wc_ref,
                 wn_ref, wup_ref, wconv_ref, wdown_ref, wfin_ref, o_ref, act_ref, halo_ref,
                 *, tm, tiles_per_seq, scw, dff, cw, final_norm):
    i = pl.program_id(0)

    @pl.when(i == 0)
    def _():
        halo_ref[...] = jnp.zeros_like(halo_ref)

    keep = i % tiles_per_seq > 0

    def causal_conv(u, w, taps):
        out = u[SUBLANES:] * w[taps - 1:taps]
        for k in range(1, taps):
            out = out + u[SUBLANES - k:SUBLANES - k + tm] * w[taps - 1 - k:taps - k]
        return out

    p = sc_ref[:, scw:2 * scw] * sc_ref[:, 2 * scw:3 * scw]
    pprev = jnp.where(keep, scprev_ref[:, scw:2 * scw] * scprev_ref[:, 2 * scw:3 * scw], 0.0)
    ysc = sc_ref[:, 0:scw] * causal_conv(jnp.concatenate([pprev, p], axis=0), wsconv_ref[...], SC_KERNEL)
    xn = x_ref[...] + jnp.dot(ysc.astype(BF16), wa_ref[...], preferred_element_type=F32)
    xn = xn + jnp.dot(yg_ref[...], wb_ref[...], preferred_element_type=F32)
    xn = xn + jnp.dot(ysb_ref[...], wc_ref[...], preferred_element_type=F32)

    xp = jnp.concatenate([jnp.where(keep, halo_ref[...], 0.0), xn], axis=0)
    halo_ref[...] = xn[tm - SUBLANES:]
    h = _rms(xp, wn_ref[...]).astype(BF16)
    for j in range(dff // cw):
        gcols = slice(j * cw, (j + 1) * cw)
        vcols = slice(dff + j * cw, dff + (j + 1) * cw)
        gate = causal_conv(jnp.dot(h, wup_ref[:, gcols], preferred_element_type=F32), wconv_ref[:, gcols], FFN_CONV)
        val = causal_conv(jnp.dot(h, wup_ref[:, vcols], preferred_element_type=F32), wconv_ref[:, vcols], FFN_CONV)
        act_ref[:, gcols] = (_silu(gate) * val).astype(BF16)
    y = xn + jnp.dot(act_ref[...], wdown_ref[...], preferred_element_type=F32)
    if final_norm:
        y = _rms(y, wfin_ref[...])
    o_ref[...] = y


def _mixffn(x2, osc, yg, ysb, wsconv, wa, wb, wc, wn, wup, wconv, wdown, wfin, seq, tm, final_norm):
    T, D = x2.shape
    scw = wa.shape[0]
    dff = wdown.shape[0]
    body = functools.partial(_mixffn_body, tm=tm, tiles_per_seq=seq // tm, scw=scw, dff=dff, cw=FFN_COLS,
                             final_norm=final_norm)
    row = lambda n: pl.BlockSpec((tm, n), lambda i: (i, 0))
    consts = (wsconv, wa, wb, wc, wn, wup, wconv, wdown, wfin)
    return pl.pallas_call(
        body,
        grid=(T // tm,),
        in_specs=[row(D), row(osc.shape[1]), pl.BlockSpec((SUBLANES, osc.shape[1]), _prev_rows_map(tm)),
                  row(yg.shape[1]), row(ysb.shape[1])] + [_const_spec(c.shape) for c in consts],
        out_specs=row(D),
        out_shape=jax.ShapeDtypeStruct((T, D), F32),
        scratch_shapes=[pltpu.VMEM((tm, dff), BF16), pltpu.VMEM((SUBLANES, D), F32)],
        compiler_params=pltpu.CompilerParams(dimension_semantics=("arbitrary",), vmem_limit_bytes=VMEM_LIMIT),
        name="mixffn",
    )(x2, osc, osc, yg, ysb, *consts)


def _pad_lanes(a):
    return jnp.pad(a, ((0, 0), (0, LANES - a.shape[1])))


def kernel(x, w_norm_mix, w_mix_in, w_sconv, w_gdn_conv, gdn_a_log, gdn_dt_bias, w_gdn_norm, w_mix_out,
           w_norm_ffn, w_ffn_up, w_ffn_conv, w_ffn_down, w_norm_final):
    batch, seq, D = x.shape
    depth = w_mix_in.shape[0]
    scw = D // 4
    gw = D // 2
    sbw = D - scw - gw
    c_g = 3 * scw
    c_ab = c_g + 4 * gw
    c_sb = c_ab + 2 * GDN_HEADS
    assert w_mix_in.shape[2] == c_sb + 3 * sbw and gw // GDN_HEADS == LANES and sbw % LANES == 0

    x2 = x.reshape(batch * seq, D).astype(F32)
    row = lambda v: v.reshape(1, -1).astype(F32)
    for l in range(depth):
        w_in = w_mix_in[l]
        wsc = w_in[:, :c_g].astype(BF16)
        wg = w_in[:, c_g:c_ab].astype(BF16)
        wab = _pad_lanes(w_in[:, c_ab:c_sb]).astype(BF16)
        wsb = w_in[:, c_sb:].astype(BF16)
        osc, og, osb, oab = _inproj(x2, row(w_norm_mix[l]), wsc, wg, wsb, wab, w_gdn_conv[l].astype(F32), seq, tm=512)

        yg = _gdn(og, oab, _pad_lanes(row(gdn_a_log[l])), _pad_lanes(row(gdn_dt_bias[l])),
                  row(w_gdn_norm[l]), batch, seq, lt=512)
        ysb = _sb(osb, batch, seq, nsub=4)

        w_out = w_mix_out[l].astype(BF16)
        x2 = _mixffn(x2, osc, yg, ysb, w_sconv[l].astype(F32), w_out[:scw], w_out[scw:scw + gw], w_out[scw + gw:],
                     row(w_norm_ffn[l]), w_ffn_up[l].astype(BF16), w_ffn_conv[l].astype(F32),
                     w_ffn_down[l].astype(BF16), row(w_norm_final), seq, tm=512, final_norm=(l == depth - 1))
    return x2.reshape(batch, seq, D).astype(x.dtype)
```

```python
import functools
import math

import jax
import jax.numpy as jnp
from jax import lax
from jax.experimental import pallas as pl
from jax.experimental.pallas import tpu as pltpu

F32 = jnp.float32
BF16 = jnp.bfloat16
NORM_EPS = 1e-6

LANES = 128
SUBLANES = 8
VMEM_LIMIT = 56 * 1024 * 1024

SC_KERNEL = 3
GDN_HEADS = 4
GDN_CONV = 4
GDN_CHUNK = 64
CONV_ROWS = 64
INV_BASE = 8
SB_HEADS = 4
SB_BLOCK = 128
FFN_CONV = 3
FFN_COLS = 256
SB_SKIP_LOG = -88.0

_NT = (((1,), (1,)), ((), ()))


def _rms(x, w):
    return x * lax.rsqrt(jnp.mean(x * x, axis=-1, keepdims=True) + NORM_EPS) * w


def _softplus(x):
    return jnp.maximum(x, 0.0) + jnp.log1p(jnp.exp(-jnp.abs(x)))


def _softplus_pos(x):
    return jnp.maximum(x, 0.0) + jnp.log(1.0 + jnp.exp(-jnp.abs(x)))


def _sigmoid(x):
    return 1.0 / (1.0 + jnp.exp(-x))


def _silu(x):
    return x * _sigmoid(x)


def _const_spec(shape):
    nd = len(shape)
    return pl.BlockSpec(shape, lambda *_: (0,) * nd, pipeline_mode=pl.Buffered(1))


def _prev_rows_map(rows_per_tile):
    step = rows_per_tile // SUBLANES
    return lambda i: (jnp.maximum(i * step - 1, 0), 0)


def _inproj_body(x_ref, wn_ref, wsc_ref, wg_ref, wsb_ref, wab_ref, wconv_ref, osc_ref, og_ref, osb_ref, oab_ref,
                 halo_ref, *, tm, tiles_per_seq, heads):
    i = pl.program_id(0)

    @pl.when(i == 0)
    def _():
        halo_ref[...] = jnp.zeros_like(halo_ref)

    h = _rms(x_ref[...], wn_ref[...]).astype(BF16)
    keep = i % tiles_per_seq > 0
    hd = LANES
    wconv = wconv_ref[...]
    def project(group):
        gcols = slice(group * heads * hd, (group + 1) * heads * hd)
        og_ref[:, gcols] = jnp.dot(h, wg_ref[:, gcols], preferred_element_type=F32)

    project(0)
    for s in range(3 * heads):
        cols = slice(s * hd, (s + 1) * hd)
        if s % heads == 0:
            project(s // heads + 1)
        above = jnp.where(keep, halo_ref[:, cols], 0.0)
        for r in range(0, tm, CONV_ROWS):
            raw = og_ref[r:r + CONV_ROWS, cols]
            xp = jnp.concatenate([above, raw], axis=0)
            above = raw[CONV_ROWS - SUBLANES:]
            conv = xp[SUBLANES:] * wconv[GDN_CONV - 1:GDN_CONV, cols]
            for k in range(1, GDN_CONV):
                conv = conv + xp[SUBLANES - k:SUBLANES - k + CONV_ROWS] * wconv[GDN_CONV - 1 - k:GDN_CONV - k, cols]
            a = _silu(conv)
            if s < 2 * heads:
                a = a * lax.rsqrt(jnp.sum(a * a, axis=-1, keepdims=True) + NORM_EPS)
            if s < heads:
                a = a * (hd ** -0.5)
            og_ref[r:r + CONV_ROWS, cols] = a
        halo_ref[:, cols] = above
    osc_ref[...] = jnp.dot(h, wsc_ref[...], preferred_element_type=F32)
    osb_ref[...] = jnp.dot(h, wsb_ref[...], preferred_element_type=F32).astype(BF16)
    oab_ref[...] = jnp.dot(h, wab_ref[...], preferred_element_type=F32)


def _inproj(x2, wn, wsc, wg, wsb, wab, wconv, seq, tm):
    T, D = x2.shape
    outs = (wsc.shape[1], wg.shape[1], wsb.shape[1], wab.shape[1])
    body = functools.partial(_inproj_body, tm=tm, tiles_per_seq=seq // tm, heads=GDN_HEADS)
    return pl.pallas_call(
        body,
        grid=(T // tm,),
        in_specs=[pl.BlockSpec((tm, D), lambda i: (i, 0)), _const_spec(wn.shape), _const_spec(wsc.shape),
                  _const_spec(wg.shape), _const_spec(wsb.shape), _const_spec(wab.shape), _const_spec(wconv.shape)],
        out_specs=[pl.BlockSpec((tm, n), lambda i: (i, 0)) for n in outs],
        out_shape=[jax.ShapeDtypeStruct((T, outs[0]), F32), jax.ShapeDtypeStruct((T, outs[1]), F32),
                   jax.ShapeDtypeStruct((T, outs[2]), BF16), jax.ShapeDtypeStruct((T, outs[3]), F32)],
        scratch_shapes=[pltpu.VMEM((SUBLANES, wconv.shape[1]), F32)],
        compiler_params=pltpu.CompilerParams(dimension_semantics=("arbitrary",), vmem_limit_bytes=VMEM_LIMIT),
        name="inproj",
    )(x2, wn, wsc, wg, wsb, wab, wconv)


def _gdn_body(g_ref, ab_ref, alog_ref, dtb_ref, wnorm_ref, y_ref, s_ref, *, lt, width, heads):
    hd = width // heads
    C = GDN_CHUNK
    nc = lt // C
    t = pl.program_id(1)

    @pl.when(t == 0)
    def _():
        s_ref[...] = jnp.zeros_like(s_ref)

    ri = lax.broadcasted_iota(jnp.int32, (C, C), 0)
    ci = lax.broadcasted_iota(jnp.int32, (C, C), 1)
    causal = ri >= ci
    strict = ri > ci
    eye = (ri == ci).astype(F32)
    ltri = causal.astype(BF16)
    neg_a = -jnp.exp(alog_ref[...])
    dtb = dtb_ref[...]
    wnorm = wnorm_ref[...]
    dot = functools.partial(jnp.dot, preferred_element_type=F32)

    tiles = [(c, h) for c in range(nc) for h in range(heads)]
    gate = []
    for c in range(nc):
        rows = slice(c * C, (c + 1) * C)
        ab = ab_ref[rows, :]
        g_all = neg_a * _softplus(ab + dtb)
        g1 = g_all.astype(BF16)
        r1 = g_all - g1.astype(F32)
        g2 = r1.astype(BF16)
        g3 = (r1 - g2.astype(F32)).astype(BF16)
        gc_all = dot(ltri, g1) + (dot(ltri, g2) + dot(ltri, g3))
        g_last = gc_all[C - 1:C, :]
        gate.append(dict(beta=_sigmoid(ab), gc=gc_all, gc_t=gc_all.T, eg=jnp.exp(gc_all),
                         ekd=jnp.exp(g_last - gc_all), eg_last=jnp.exp(g_last)))

    ks, qs, decays, kbs, rhss = [], [], [], [], []
    for c, h in tiles:
        gt = gate[c]
        rows = slice(c * C, (c + 1) * C)
        q = g_ref[rows, h * hd:(h + 1) * hd]
        k = g_ref[rows, (heads + h) * hd:(heads + h + 1) * hd]
        v = g_ref[rows, (2 * heads + h) * hd:(2 * heads + h + 1) * hd]
        beta = gt["beta"][:, heads + h:heads + h + 1]
        diff = gt["gc"][:, h:h + 1] - gt["gc_t"][h:h + 1, :]
        decays.append(jnp.where(causal, jnp.exp(jnp.where(causal, diff, 0.0)), 0.0))
        kb = k * beta
        ks.append(k)
        qs.append(q)
        kbs.append(kb)
        rhss.append(jnp.concatenate([v * beta, kb * gt["eg"][:, h:h + 1]], axis=1))
    kk = [lax.dot_general(kb.astype(BF16), k.astype(BF16), _NT, preferred_element_type=F32) for kb, k in zip(kbs, ks)]
    qk = [lax.dot_general(q.astype(BF16), k.astype(BF16), _NT, preferred_element_type=F32) for q, k in zip(qs, ks)]
    lms = [jnp.where(strict, x * d, 0.0) for x, d in zip(kk, decays)]
    attns = [(x * d).astype(BF16) for x, d in zip(qk, decays)]

    def same_block(n):
        shift = n.bit_length() - 1
        return jnp.right_shift(ri, shift) == jnp.right_shift(ci, shift)

    def bdot(a, b):
        return dot(a.astype(BF16), b.astype(BF16))

    inner = same_block(INV_BASE)
    ms = [jnp.where(inner, lm, 0.0) for lm in lms]
    ps = [eye - d for d in ms]
    span = 2
    while span < INV_BASE:
        ms = [bdot(m, m) for m in ms]
        ps = [p + bdot(p, m) for p, m in zip(ps, ms)]
        span *= 2
    size = INV_BASE
    while size < C:
        outer = same_block(2 * size)
        below = jnp.logical_and(outer, jnp.logical_not(inner))
        bt = [bdot(jnp.where(below, lm, 0.0), p) for lm, p in zip(lms, ps)]
        ps = [p - bdot(p, x) for p, x in zip(ps, bt)]
        inner = outer
        size *= 2
    sols = [bdot(p, r) for p, r in zip(ps, rhss)]

    for c in range(nc):
        gt = gate[c]
        rows = slice(c * C, (c + 1) * C)
        idx = [c * heads + h for h in range(heads)]
        states = [s_ref[h] for h in range(heads)]
        wq = [jnp.concatenate([sols[i][:, hd:], qs[i] * gt["eg"][:, h:h + 1]], axis=0).astype(BF16)
              for h, i in enumerate(idx)]
        kd_t = [(ks[i] * gt["ekd"][:, h:h + 1]).T.astype(BF16) for h, i in enumerate(idx)]
        wq_s = [dot(wq[h], states[h].astype(BF16)) for h in range(heads)]
        v_new = [(sols[i][:, :hd] - wq_s[h][:C]).astype(BF16) for h, i in enumerate(idx)]
        for h in range(heads):
            s_ref[h] = states[h] * gt["eg_last"][:, h:h + 1] + dot(kd_t[h], v_new[h])
        for h, i in enumerate(idx):
            o = wq_s[h][C:] + dot(attns[i], v_new[h])
            z = g_ref[rows, (3 * heads + h) * hd:(3 * heads + h + 1) * hd]
            y_ref[rows, h * hd:(h + 1) * hd] = (_rms(o, wnorm) * _silu(z)).astype(y_ref.dtype)


def _gdn(og, oab, alog, dtb, wnorm, batch, seq, lt):
    T, gw = og.shape
    width = gw // 4
    heads = GDN_HEADS
    nt = seq // lt
    body = functools.partial(_gdn_body, lt=lt, width=width, heads=heads)
    return pl.pallas_call(
        body,
        grid=(batch, nt),
        in_specs=[pl.BlockSpec((lt, gw), lambda b, t: (b * nt + t, 0)),
                  pl.BlockSpec((lt, LANES), lambda b, t: (b * nt + t, 0)),
                  pl.BlockSpec(alog.shape, lambda b, t: (0, 0)),
                  pl.BlockSpec(dtb.shape, lambda b, t: (0, 0)),
                  pl.BlockSpec(wnorm.shape, lambda b, t: (0, 0))],
        out_specs=pl.BlockSpec((lt, width), lambda b, t: (b * nt + t, 0)),
        out_shape=jax.ShapeDtypeStruct((T, width), BF16),
        scratch_shapes=[pltpu.VMEM((heads, width // heads, width // heads), F32)],
        compiler_params=pltpu.CompilerParams(dimension_semantics=("arbitrary", "arbitrary"),
                                             vmem_limit_bytes=VMEM_LIMIT),
        name="gdn",
    )(og, oab, alog, dtb, wnorm)


def _sb_body(q_ref, k_ref, v_ref, o_ref, *, nsub, hd):
    bq = SB_BLOCK
    step = pl.program_id(2)
    scale = hd ** -0.5
    scale_on_q = math.frexp(scale)[0] == 0.5
    lane = lax.broadcasted_iota(jnp.int32, (1, LANES), 1)
    first_head = lane < hd

    def later(n):
        return (lax.broadcasted_iota(jnp.int32, (n, n), 0) > lax.broadcasted_iota(jnp.int32, (n, n), 1)).astype(BF16)

    def keys_after(lom, later_m):
        hi = lom.astype(BF16)
        lo = (lom - hi.astype(F32)).astype(BF16)
        return jnp.dot(hi, later_m, preferred_element_type=F32) + jnp.dot(lo, later_m, preferred_element_type=F32)

    def stack_heads(q):
        if scale_on_q:
            q = q * jnp.asarray(scale, q.dtype)
        zero = jnp.zeros_like(q)
        return jnp.concatenate([jnp.where(first_head, q, zero), jnp.where(first_head, zero, q)], axis=0)

    row = lax.broadcasted_iota(jnp.int32, (2 * bq, 2 * bq), 0)
    ahead = lax.broadcasted_iota(jnp.int32, (2 * bq, 2 * bq), 1) - jnp.where(row >= bq, row - bq, row)
    later2 = later(2 * bq)

    def logits(qq, rows):
        z = lax.dot_general(qq, k_ref[rows, :], _NT, preferred_element_type=F32)
        return z if scale_on_q else z * scale

    blocks = [step * nsub + s for s in range(nsub)]
    starts = [jnp.maximum(i - 1, 0) * bq for i in blocks]
    qqs = [stack_heads(q_ref[s * bq:(s + 1) * bq, :]) for s in range(nsub)]
    wins = [pl.ds(pl.multiple_of(w, bq), 2 * bq) for w in starts]
    zs = [logits(qq, win) for qq, win in zip(qqs, wins)]
    stricts = [ahead < i * bq - w for i, w in zip(blocks, starts)]
    loms = [jnp.where(st, -_softplus_pos(z), 0.0) for st, z in zip(stricts, zs)]
    tails = [keys_after(lom, later2) for lom in loms]
    weights = [jnp.where(st, jnp.exp(z + lom + tail), 0.0).astype(BF16)
               for st, z, lom, tail in zip(stricts, zs, loms, tails)]
    accs = [jnp.dot(a, v_ref[win, :], preferred_element_type=F32) for a, win in zip(weights, wins)]
    carries = [jnp.sum(lom, axis=-1, keepdims=True) for lom in loms]

    later1 = later(bq)

    def cond(state):
        j, carry, _ = state
        return jnp.logical_and(j >= 0, jnp.max(carry) > SB_SKIP_LOG)

    for s in range(nsub):
        qq = qqs[s]

        def body(state, qq=qq):
            j, carry, acc = state
            rows = pl.ds(pl.multiple_of(j * bq, bq), bq)
            z = logits(qq, rows)
            lom = -_softplus_pos(z)
            a = jnp.exp(z + lom + keys_after(lom, later1) + carry)
            acc = acc + jnp.dot(a.astype(BF16), v_ref[rows, :], preferred_element_type=F32)
            return j - 1, carry + jnp.sum(lom, axis=-1, keepdims=True), acc

        _, _, acc = lax.while_loop(cond, body, (blocks[s] - 2, carries[s], accs[s]))
        o_ref[s * bq:(s + 1) * bq, :] = jnp.where(first_head, acc[:bq], acc[bq:]).astype(o_ref.dtype)


def _sb(osb, batch, seq, nsub):
    T, w3 = osb.shape
    width = w3 // 3
    pairs = width // LANES
    hd = width // SB_HEADS
    tq = nsub * SB_BLOCK
    nq = seq // tq
    assert seq % tq == 0 and seq >= 2 * SB_BLOCK
    body = functools.partial(_sb_body, nsub=nsub, hd=hd)
    return pl.pallas_call(
        body,
        grid=(batch, pairs, nq),
        in_specs=[pl.BlockSpec((tq, LANES), lambda b, p, i: (b * nq + i, p)),
                  pl.BlockSpec((seq, LANES), lambda b, p, i: (b, pairs + p)),
                  pl.BlockSpec((seq, LANES), lambda b, p, i: (b, 2 * pairs + p))],
        out_specs=pl.BlockSpec((tq, LANES), lambda b, p, i: (b * nq + i, p)),
        out_shape=jax.ShapeDtypeStruct((T, width), BF16),
        compiler_params=pltpu.CompilerParams(dimension_semantics=("arbitrary", "arbitrary", "arbitrary"),
                                             vmem_limit_bytes=VMEM_LIMIT),
        name="stickbreak",
    )(osb, osb, osb)


def _mixffn_body(x_ref, sc_ref, scprev_ref, yg_ref, ysb_ref, wsconv_ref, wa_ref, wb_ref, wc_ref,
                 wn_ref, wup_ref, wconv_ref, wdown_ref, wfin_ref, o_ref, act_ref, halo_ref,
                 *, tm, tiles_per_seq, scw, dff, cw, final_norm):
    i = pl.program_id(0)

    @pl.when(i == 0)
    def _():
        halo_ref[...] = jnp.zeros_like(halo_ref)

    keep = i % tiles_per_seq > 0

    def causal_conv(u, w, taps):
        out = u[SUBLANES:] * w[taps - 1:taps]
        for k in range(1, taps):
            out = out + u[SUBLANES - k:SUBLANES - k + tm] * w[taps - 1 - k:taps - k]
        return out

    p = sc_ref[:, scw:2 * scw] * sc_ref[:, 2 * scw:3 * scw]
    pprev = jnp.where(keep, scprev_ref[:, scw:2 * scw] * scprev_ref[:, 2 * scw:3 * scw], 0.0)
    ysc = sc_ref[:, 0:scw] * causal_conv(jnp.concatenate([pprev, p], axis=0), wsconv_ref[...], SC_KERNEL)
    xn = x_ref[...] + jnp.dot(ysc.astype(BF16), wa_ref[...], preferred_element_type=F32)
    xn = xn + jnp.dot(yg_ref[...], wb_ref[...], preferred_element_type=F32)
    xn = xn + jnp.dot(ysb_ref[...], wc_ref[...], preferred_element_type=F32)

    xp = jnp.concatenate([jnp.where(keep, halo_ref[...], 0.0), xn], axis=0)
    halo_ref[...] = xn[tm - SUBLANES:]
    h = _rms(xp, wn_ref[...]).astype(BF16)
    for j in range(dff // cw):
        gcols = slice(j * cw, (j + 1) * cw)
        vcols = slice(dff + j * cw, dff + (j + 1) * cw)
        gate = causal_conv(jnp.dot(h, wup_ref[:, gcols], preferred_element_type=F32), wconv_ref[:, gcols], FFN_CONV)
        val = causal_conv(jnp.dot(h, wup_ref[:, vcols], preferred_element_type=F32), wconv_ref[:, vcols], FFN_CONV)
        act_ref[:, gcols] = (_silu(gate) * val).astype(BF16)
    y = xn + jnp.dot(act_ref[...], wdown_ref[...], preferred_element_type=F32)
    if final_norm:
        y = _rms(y, wfin_ref[...])
    o_ref[...] = y


def _mixffn(x2, osc, yg, ysb, wsconv, wa, wb, wc, wn, wup, wconv, wdown, wfin, seq, tm, final_norm):
    T, D = x2.shape
    scw = wa.shape[0]
    dff = wdown.shape[0]
    body = functools.partial(_mixffn_body, tm=tm, tiles_per_seq=seq // tm, scw=scw, dff=dff, cw=FFN_COLS,
                             final_norm=final_norm)
    row = lambda n: pl.BlockSpec((tm, n), lambda i: (i, 0))
    consts = (wsconv, wa, wb, wc, wn, wup, wconv, wdown, wfin)
    return pl.pallas_call(
        body,
        grid=(T // tm,),
        in_specs=[row(D), row(osc.shape[1]), pl.BlockSpec((SUBLANES, osc.shape[1]), _prev_rows_map(tm)),
                  row(yg.shape[1]), row(ysb.shape[1])] + [_const_spec(c.shape) for c in consts],
        out_specs=row(D),
        out_shape=jax.ShapeDtypeStruct((T, D), F32),
        scratch_shapes=[pltpu.VMEM((tm, dff), BF16), pltpu.VMEM((SUBLANES, D), F32)],
        compiler_params=pltpu.CompilerParams(dimension_semantics=("arbitrary",), vmem_limit_bytes=VMEM_LIMIT),
        name="mixffn",
    )(x2, osc, osc, yg, ysb, *consts)


def _pad_lanes(a):
    return jnp.pad(a, ((0, 0), (0, LANES - a.shape[1])))


def kernel(x, w_norm_mix, w_mix_in, w_sconv, w_gdn_conv, gdn_a_log, gdn_dt_bias, w_gdn_norm, w_mix_out,
           w_norm_ffn, w_ffn_up, w_ffn_conv, w_ffn_down, w_norm_final):
    batch, seq, D = x.shape
    depth = w_mix_in.shape[0]
    scw = D // 4
    gw = D // 2
    sbw = D - scw - gw
    c_g = 3 * scw
    c_ab = c_g + 4 * gw
    c_sb = c_ab + 2 * GDN_HEADS
    assert w_mix_in.shape[2] == c_sb + 3 * sbw and gw // GDN_HEADS == LANES and sbw % LANES == 0

    x2 = x.reshape(batch * seq, D).astype(F32)
    row = lambda v: v.reshape(1, -1).astype(F32)
    for l in range(depth):
        w_in = w_mix_in[l]
        wsc = w_in[:, :c_g].astype(BF16)
        wg = w_in[:, c_g:c_ab].astype(BF16)
        wab = _pad_lanes(w_in[:, c_ab:c_sb]).astype(BF16)
        wsb = w_in[:, c_sb:].astype(BF16)
        osc, og, osb, oab = _inproj(x2, row(w_norm_mix[l]), wsc, wg, wsb, wab, w_gdn_conv[l].astype(F32), seq, tm=512)

        yg = _gdn(og, oab, _pad_lanes(row(gdn_a_log[l])), _pad_lanes(row(gdn_dt_bias[l])),
                  row(w_gdn_norm[l]), batch, seq, lt=512)
        ysb = _sb(osb, batch, seq, nsub=4)

        w_out = w_mix_out[l].astype(BF16)
        x2 = _mixffn(x2, osc, yg, ysb, w_sconv[l].astype(F32), w_out[:scw], w_out[scw:scw + gw], w_out[scw + gw:],
                     row(w_norm_ffn[l]), w_ffn_up[l].astype(BF16), w_ffn_conv[l].astype(F32),
                     w_ffn_down[l].astype(BF16), row(w_norm_final), seq, tm=512, final_norm=(l == depth - 1))
    return x2.reshape(batch, seq, D).astype(x.dtype)
```

```python
import functools
import math

import jax
import jax.numpy as jnp
from jax import lax
from jax.experimental import pallas as pl
from jax.experimental.pallas import tpu as pltpu

F32 = jnp.float32
BF16 = jnp.bfloat16
NORM_EPS = 1e-6

LANES = 128
SUBLANES = 8
VMEM_LIMIT = 56 * 1024 * 1024

SC_KERNEL = 3
GDN_HEADS = 4
GDN_CONV = 4
GDN_CHUNK = 64
INV_BASE = 8
SB_HEADS = 4
SB_BLOCK = 128
FFN_CONV = 3
FFN_COLS = 256
SB_SKIP_LOG = -88.0

_NT = (((1,), (1,)), ((), ()))


def _rms(x, w):
    return x * lax.rsqrt(jnp.mean(x * x, axis=-1, keepdims=True) + NORM_EPS) * w


def _softplus(x):
    return jnp.maximum(x, 0.0) + jnp.log1p(jnp.exp(-jnp.abs(x)))


def _softplus_pos(x):
    return jnp.maximum(x, 0.0) + jnp.log(1.0 + jnp.exp(-jnp.abs(x)))


def _sigmoid(x):
    return 1.0 / (1.0 + jnp.exp(-x))


def _silu(x):
    return x * _sigmoid(x)


def _const_spec(shape):
    nd = len(shape)
    return pl.BlockSpec(shape, lambda *_: (0,) * nd, pipeline_mode=pl.Buffered(1))


def _prev_rows_map(rows_per_tile):
    step = rows_per_tile // SUBLANES
    return lambda i: (jnp.maximum(i * step - 1, 0), 0)


def _inproj_body(x_ref, wn_ref, wsc_ref, wg_ref, wsb_ref, wab_ref, wconv_ref, osc_ref, og_ref, osb_ref, oab_ref,
                 halo_ref, *, tm, tiles_per_seq, heads):
    i = pl.program_id(0)

    @pl.when(i == 0)
    def _():
        halo_ref[...] = jnp.zeros_like(halo_ref)

    h = _rms(x_ref[...], wn_ref[...]).astype(BF16)
    keep = i % tiles_per_seq > 0
    hd = LANES
    wconv = wconv_ref[...]
    def project(group):
        gcols = slice(group * heads * hd, (group + 1) * heads * hd)
        og_ref[:, gcols] = jnp.dot(h, wg_ref[:, gcols], preferred_element_type=F32)

    project(0)
    for s in range(3 * heads):
        cols = slice(s * hd, (s + 1) * hd)
        if s % heads == 0:
            project(s // heads + 1)
        raw = og_ref[:, cols]
        xp = jnp.concatenate([jnp.where(keep, halo_ref[:, cols], 0.0), raw], axis=0)
        halo_ref[:, cols] = raw[tm - SUBLANES:]
        conv = xp[SUBLANES:] * wconv[GDN_CONV - 1:GDN_CONV, cols]
        for k in range(1, GDN_CONV):
            conv = conv + xp[SUBLANES - k:SUBLANES - k + tm] * wconv[GDN_CONV - 1 - k:GDN_CONV - k, cols]
        a = _silu(conv)
        if s < 2 * heads:
            inv = lax.rsqrt(jnp.sum(a * a, axis=-1, keepdims=True) + NORM_EPS)
            a = a * (inv * (hd ** -0.5) if s < heads else inv)
        og_ref[:, cols] = a
    osc_ref[...] = jnp.dot(h, wsc_ref[...], preferred_element_type=F32)
    osb_ref[...] = jnp.dot(h, wsb_ref[...], preferred_element_type=F32).astype(BF16)
    oab_ref[...] = jnp.dot(h, wab_ref[...], preferred_element_type=F32)


def _inproj(x2, wn, wsc, wg, wsb, wab, wconv, seq, tm):
    T, D = x2.shape
    outs = (wsc.shape[1], wg.shape[1], wsb.shape[1], wab.shape[1])
    body = functools.partial(_inproj_body, tm=tm, tiles_per_seq=seq // tm, heads=GDN_HEADS)
    return pl.pallas_call(
        body,
        grid=(T // tm,),
        in_specs=[pl.BlockSpec((tm, D), lambda i: (i, 0)), _const_spec(wn.shape), _const_spec(wsc.shape),
                  _const_spec(wg.shape), _const_spec(wsb.shape), _const_spec(wab.shape), _const_spec(wconv.shape)],
        out_specs=[pl.BlockSpec((tm, n), lambda i: (i, 0)) for n in outs],
        out_shape=[jax.ShapeDtypeStruct((T, outs[0]), F32), jax.ShapeDtypeStruct((T, outs[1]), F32),
                   jax.ShapeDtypeStruct((T, outs[2]), BF16), jax.ShapeDtypeStruct((T, outs[3]), F32)],
        scratch_shapes=[pltpu.VMEM((SUBLANES, wconv.shape[1]), F32)],
        compiler_params=pltpu.CompilerParams(dimension_semantics=("arbitrary",), vmem_limit_bytes=VMEM_LIMIT),
        name="inproj",
    )(x2, wn, wsc, wg, wsb, wab, wconv)


def _gdn_body(g_ref, ab_ref, alog_ref, dtb_ref, wnorm_ref, y_ref, s_ref, *, lt, width, heads):
    hd = width // heads
    C = GDN_CHUNK
    nc = lt // C
    t = pl.program_id(1)

    @pl.when(t == 0)
    def _():
        s_ref[...] = jnp.zeros_like(s_ref)

    ri = lax.broadcasted_iota(jnp.int32, (C, C), 0)
    ci = lax.broadcasted_iota(jnp.int32, (C, C), 1)
    causal = ri >= ci
    strict = ri > ci
    eye = (ri == ci).astype(F32)
    ltri = causal.astype(BF16)
    neg_a = -jnp.exp(alog_ref[...])
    dtb = dtb_ref[...]
    wnorm = wnorm_ref[...]
    dot = functools.partial(jnp.dot, preferred_element_type=F32)

    tiles = [(c, h) for c in range(nc) for h in range(heads)]
    gate = []
    for c in range(nc):
        rows = slice(c * C, (c + 1) * C)
        ab = ab_ref[rows, :]
        g_all = neg_a * _softplus(ab + dtb)
        g1 = g_all.astype(BF16)
        r1 = g_all - g1.astype(F32)
        g2 = r1.astype(BF16)
        g3 = (r1 - g2.astype(F32)).astype(BF16)
        gc_all = dot(ltri, g1) + (dot(ltri, g2) + dot(ltri, g3))
        g_last = gc_all[C - 1:C, :]
        gate.append(dict(beta=_sigmoid(ab), gc=gc_all, gc_t=gc_all.T, eg=jnp.exp(gc_all),
                         ekd=jnp.exp(g_last - gc_all), eg_last=jnp.exp(g_last)))

    ks, qs, decays, kbs, rhss = [], [], [], [], []
    for c, h in tiles:
        gt = gate[c]
        rows = slice(c * C, (c + 1) * C)
        q = g_ref[rows, h * hd:(h + 1) * hd]
        k = g_ref[rows, (heads + h) * hd:(heads + h + 1) * hd]
        v = g_ref[rows, (2 * heads + h) * hd:(2 * heads + h + 1) * hd]
        beta = gt["beta"][:, heads + h:heads + h + 1]
        diff = gt["gc"][:, h:h + 1] - gt["gc_t"][h:h + 1, :]
        decays.append(jnp.where(causal, jnp.exp(jnp.where(causal, diff, 0.0)), 0.0))
        kb = k * beta
        ks.append(k)
        qs.append(q)
        kbs.append(kb)
        rhss.append(jnp.concatenate([v * beta, kb * gt["eg"][:, h:h + 1]], axis=1))
    kk = [lax.dot_general(kb.astype(BF16), k.astype(BF16), _NT, preferred_element_type=F32) for kb, k in zip(kbs, ks)]
    qk = [lax.dot_general(q.astype(BF16), k.astype(BF16), _NT, preferred_element_type=F32) for q, k in zip(qs, ks)]
    lms = [jnp.where(strict, x * d, 0.0) for x, d in zip(kk, decays)]
    attns = [(x * d).astype(BF16) for x, d in zip(qk, decays)]

    def same_block(n):
        shift = n.bit_length() - 1
        return jnp.right_shift(ri, shift) == jnp.right_shift(ci, shift)

    def bdot(a, b):
        return dot(a.astype(BF16), b.astype(BF16))

    inner = same_block(INV_BASE)
    ms = [jnp.where(inner, lm, 0.0) for lm in lms]
    ps = [eye - d for d in ms]
    span = 2
    while span < INV_BASE:
        ms = [bdot(m, m) for m in ms]
        ps = [p + bdot(p, m) for p, m in zip(ps, ms)]
        span *= 2
    size = INV_BASE
    while size < C:
        outer = same_block(2 * size)
        below = jnp.logical_and(outer, jnp.logical_not(inner))
        bt = [bdot(jnp.where(below, lm, 0.0), p) for lm, p in zip(lms, ps)]
        ps = [p - bdot(p, x) for p, x in zip(ps, bt)]
        inner = outer
        size *= 2
    sols = [bdot(p, r) for p, r in zip(ps, rhss)]

    for c in range(nc):
        gt = gate[c]
        rows = slice(c * C, (c + 1) * C)
        idx = [c * heads + h for h in range(heads)]
        states = [s_ref[h] for h in range(heads)]
        wq = [jnp.concatenate([sols[i][:, hd:], qs[i] * gt["eg"][:, h:h + 1]], axis=0).astype(BF16)
              for h, i in enumerate(idx)]
        kd_t = [(ks[i] * gt["ekd"][:, h:h + 1]).T.astype(BF16) for h, i in enumerate(idx)]
        wq_s = [dot(wq[h], states[h].astype(BF16)) for h in range(heads)]
        v_new = [(sols[i][:, :hd] - wq_s[h][:C]).astype(BF16) for h, i in enumerate(idx)]
        for h in range(heads):
            s_ref[h] = states[h] * gt["eg_last"][:, h:h + 1] + dot(kd_t[h], v_new[h])
        for h, i in enumerate(idx):
            o = wq_s[h][C:] + dot(attns[i], v_new[h])
            z = g_ref[rows, (3 * heads + h) * hd:(3 * heads + h + 1) * hd]
            y_ref[rows, h * hd:(h + 1) * hd] = (_rms(o, wnorm) * _silu(z)).astype(y_ref.dtype)


def _gdn(og, oab, alog, dtb, wnorm, batch, seq, lt):
    T, gw = og.shape
    width = gw // 4
    heads = GDN_HEADS
    nt = seq // lt
    body = functools.partial(_gdn_body, lt=lt, width=width, heads=heads)
    return pl.pallas_call(
        body,
        grid=(batch, nt),
        in_specs=[pl.BlockSpec((lt, gw), lambda b, t: (b * nt + t, 0)),
                  pl.BlockSpec((lt, LANES), lambda b, t: (b * nt + t, 0)),
                  pl.BlockSpec(alog.shape, lambda b, t: (0, 0)),
                  pl.BlockSpec(dtb.shape, lambda b, t: (0, 0)),
                  pl.BlockSpec(wnorm.shape, lambda b, t: (0, 0))],
        out_specs=pl.BlockSpec((lt, width), lambda b, t: (b * nt + t, 0)),
        out_shape=jax.ShapeDtypeStruct((T, width), BF16),
        scratch_shapes=[pltpu.VMEM((heads, width // heads, width // heads), F32)],
        compiler_params=pltpu.CompilerParams(dimension_semantics=("arbitrary", "arbitrary"),
                                             vmem_limit_bytes=VMEM_LIMIT),
        name="gdn",
    )(og, oab, alog, dtb, wnorm)


def _sb_body(q_ref, k_ref, v_ref, o_ref, *, nsub, hd):
    bq = SB_BLOCK
    step = pl.program_id(2)
    scale = hd ** -0.5
    scale_on_q = math.frexp(scale)[0] == 0.5
    lane = lax.broadcasted_iota(jnp.int32, (1, LANES), 1)
    first_head = lane < hd

    def later(n):
        return (lax.broadcasted_iota(jnp.int32, (n, n), 0) > lax.broadcasted_iota(jnp.int32, (n, n), 1)).astype(BF16)

    def keys_after(lom, later_m):
        hi = lom.astype(BF16)
        lo = (lom - hi.astype(F32)).astype(BF16)
        return jnp.dot(hi, later_m, preferred_element_type=F32) + jnp.dot(lo, later_m, preferred_element_type=F32)

    def stack_heads(q):
        if scale_on_q:
            q = q * jnp.asarray(scale, q.dtype)
        zero = jnp.zeros_like(q)
        return jnp.concatenate([jnp.where(first_head, q, zero), jnp.where(first_head, zero, q)], axis=0)

    row = lax.broadcasted_iota(jnp.int32, (2 * bq, 2 * bq), 0)
    ahead = lax.broadcasted_iota(jnp.int32, (2 * bq, 2 * bq), 1) - jnp.where(row >= bq, row - bq, row)
    later2 = later(2 * bq)

    def logits(qq, rows):
        z = lax.dot_general(qq, k_ref[rows, :], _NT, preferred_element_type=F32)
        return z if scale_on_q else z * scale

    blocks = [step * nsub + s for s in range(nsub)]
    starts = [jnp.maximum(i - 1, 0) * bq for i in blocks]
    qqs = [stack_heads(q_ref[s * bq:(s + 1) * bq, :]) for s in range(nsub)]
    wins = [pl.ds(pl.multiple_of(w, bq), 2 * bq) for w in starts]
    zs = [logits(qq, win) for qq, win in zip(qqs, wins)]
    stricts = [ahead < i * bq - w for i, w in zip(blocks, starts)]
    loms = [jnp.where(st, -_softplus_pos(z), 0.0) for st, z in zip(stricts, zs)]
    tails = [keys_after(lom, later2) for lom in loms]
    weights = [jnp.where(st, jnp.exp(z + lom + tail), 0.0).astype(BF16)
               for st, z, lom, tail in zip(stricts, zs, loms, tails)]
    accs = [jnp.dot(a, v_ref[win, :], preferred_element_type=F32) for a, win in zip(weights, wins)]
    carries = [jnp.sum(lom, axis=-1, keepdims=True) for lom in loms]

    later1 = later(bq)

    def cond(state):
        j, carry, _ = state
        return jnp.logical_and(j >= 0, jnp.max(carry) > SB_SKIP_LOG)

    for s in range(nsub):
        qq = qqs[s]

        def body(state, qq=qq):
            j, carry, acc = state
            rows = pl.ds(pl.multiple_of(j * bq, bq), bq)
            z = logits(qq, rows)
            lom = -_softplus_pos(z)
            a = jnp.exp(z + lom + keys_after(lom, later1) + carry)
            acc = acc + jnp.dot(a.astype(BF16), v_ref[rows, :], preferred_element_type=F32)
            return j - 1, carry + jnp.sum(lom, axis=-1, keepdims=True), acc

        _, _, acc = lax.while_loop(cond, body, (blocks[s] - 2, carries[s], accs[s]))
        o_ref[s * bq:(s + 1) * bq, :] = jnp.where(first_head, acc[:bq], acc[bq:]).astype(o_ref.dtype)


def _sb(osb, batch, seq, nsub):
    T, w3 = osb.shape
    width = w3 // 3
    pairs = width // LANES
    hd = width // SB_HEADS
    tq = nsub * SB_BLOCK
    nq = seq // tq
    assert seq % tq == 0 and seq >= 2 * SB_BLOCK
    body = functools.partial(_sb_body, nsub=nsub, hd=hd)
    return pl.pallas_call(
        body,
        grid=(batch, pairs, nq),
        in_specs=[pl.BlockSpec((tq, LANES), lambda b, p, i: (b * nq + i, p)),
                  pl.BlockSpec((seq, LANES), lambda b, p, i: (b, pairs + p)),
                  pl.BlockSpec((seq, LANES), lambda b, p, i: (b, 2 * pairs + p))],
        out_specs=pl.BlockSpec((tq, LANES), lambda b, p, i: (b * nq + i, p)),
        out_shape=jax.ShapeDtypeStruct((T, width), BF16),
        compiler_params=pltpu.CompilerParams(dimension_semantics=("arbitrary", "arbitrary", "arbitrary"),
                                             vmem_limit_bytes=VMEM_LIMIT),
        name="stickbreak",
    )(osb, osb, osb)


def _mixffn_body(x_ref, sc_ref, scprev_ref, yg_ref, ysb_ref, wsconv_ref, wa_ref, wb_ref, wc_ref,
                 wn_ref, wup_ref, wconv_ref, wdown_ref, wfin_ref, o_ref, act_ref, halo_ref,
                 *, tm, tiles_per_seq, scw, dff, cw, final_norm):
    i = pl.program_id(0)

    @pl.when(i == 0)
    def _():
        halo_ref[...] = jnp.zeros_like(halo_ref)

    keep = i % tiles_per_seq > 0

    def causal_conv(u, w, taps):
        out = u[SUBLANES:] * w[taps - 1:taps]
        for k in range(1, taps):
            out = out + u[SUBLANES - k:SUBLANES - k + tm] * w[taps - 1 - k:taps - k]
        return out

    p = sc_ref[:, scw:2 * scw] * sc_ref[:, 2 * scw:3 * scw]
    pprev = jnp.where(keep, scprev_ref[:, scw:2 * scw] * scprev_ref[:, 2 * scw:3 * scw], 0.0)
    ysc = sc_ref[:, 0:scw] * causal_conv(jnp.concatenate([pprev, p], axis=0), wsconv_ref[...], SC_KERNEL)
    xn = x_ref[...] + jnp.dot(ysc.astype(BF16), wa_ref[...], preferred_element_type=F32)
    xn = xn + jnp.dot(yg_ref[...], wb_ref[...], preferred_element_type=F32)
    xn = xn + jnp.dot(ysb_ref[...], wc_ref[...], preferred_element_type=F32)

    h = _rms(xn, wn_ref[...]).astype(BF16)

    def up_conv(cols):
        u = jnp.dot(h, wup_ref[:, cols], preferred_element_type=F32)
        above = jnp.where(keep, halo_ref[:, cols], 0.0)
        halo_ref[:, cols] = u[tm - SUBLANES:]
        return causal_conv(jnp.concatenate([above, u], axis=0), wconv_ref[:, cols], FFN_CONV)

    for j in range(dff // cw):
        gcols = slice(j * cw, (j + 1) * cw)
        gate = up_conv(gcols)
        val = up_conv(slice(dff + j * cw, dff + (j + 1) * cw))
        act_ref[:, gcols] = (_silu(gate) * val).astype(BF16)
    y = xn + jnp.dot(act_ref[...], wdown_ref[...], preferred_element_type=F32)
    if final_norm:
        y = _rms(y, wfin_ref[...])
    o_ref[...] = y


def _mixffn(x2, osc, yg, ysb, wsconv, wa, wb, wc, wn, wup, wconv, wdown, wfin, seq, tm, final_norm):
    T, D = x2.shape
    scw = wa.shape[0]
    dff = wdown.shape[0]
    body = functools.partial(_mixffn_body, tm=tm, tiles_per_seq=seq // tm, scw=scw, dff=dff, cw=FFN_COLS,
                             final_norm=final_norm)
    row = lambda n: pl.BlockSpec((tm, n), lambda i: (i, 0))
    consts = (wsconv, wa, wb, wc, wn, wup, wconv, wdown, wfin)
    return pl.pallas_call(
        body,
        grid=(T // tm,),
        in_specs=[row(D), row(osc.shape[1]), pl.BlockSpec((SUBLANES, osc.shape[1]), _prev_rows_map(tm)),
                  row(yg.shape[1]), row(ysb.shape[1])] + [_const_spec(c.shape) for c in consts],
        out_specs=row(D),
        out_shape=jax.ShapeDtypeStruct((T, D), F32),
        scratch_shapes=[pltpu.VMEM((tm, dff), BF16), pltpu.VMEM((SUBLANES, 2 * dff), F32)],
        compiler_params=pltpu.CompilerParams(dimension_semantics=("arbitrary",), vmem_limit_bytes=VMEM_LIMIT),
        name="mixffn",
    )(x2, osc, osc, yg, ysb, *consts)


def _pad_lanes(a):
    return jnp.pad(a, ((0, 0), (0, LANES - a.shape[1])))


def kernel(x, w_norm_mix, w_mix_in, w_sconv, w_gdn_conv, gdn_a_log, gdn_dt_bias, w_gdn_norm, w_mix_out,
           w_norm_ffn, w_ffn_up, w_ffn_conv, w_ffn_down, w_norm_final):
    batch, seq, D = x.shape
    depth = w_mix_in.shape[0]
    scw = D // 4
    gw = D // 2
    sbw = D - scw - gw
    c_g = 3 * scw
    c_ab = c_g + 4 * gw
    c_sb = c_ab + 2 * GDN_HEADS
    assert w_mix_in.shape[2] == c_sb + 3 * sbw and gw // GDN_HEADS == LANES and sbw % LANES == 0

    x2 = x.reshape(batch * seq, D).astype(F32)
    row = lambda v: v.reshape(1, -1).astype(F32)
    for l in range(depth):
        w_in = w_mix_in[l]
        wsc = w_in[:, :c_g].astype(BF16)
        wg = w_in[:, c_g:c_ab].astype(BF16)
        wab = _pad_lanes(w_in[:, c_ab:c_sb]).astype(BF16)
        wsb = w_in[:, c_sb:].astype(BF16)
        osc, og, osb, oab = _inproj(x2, row(w_norm_mix[l]), wsc, wg, wsb, wab, w_gdn_conv[l].astype(F32), seq, tm=512)

        yg = _gdn(og, oab, _pad_lanes(row(gdn_a_log[l])), _pad_lanes(row(gdn_dt_bias[l])),
                  row(w_gdn_norm[l]), batch, seq, lt=512)
        ysb = _sb(osb, batch, seq, nsub=8)

        w_out = w_mix_out[l].astype(BF16)
        x2 = _mixffn(x2, osc, yg, ysb, w_sconv[l].astype(F32), w_out[:scw], w_out[scw:scw + gw], w_out[scw + gw:],
                     row(w_norm_ffn[l]), w_ffn_up[l].astype(BF16), w_ffn_conv[l].astype(F32),
                     w_ffn_down[l].astype(BF16), row(w_norm_final), seq, tm=512, final_norm=(l == depth - 1))
    return x2.reshape(batch, seq, D).astype(x.dtype)
```

```python
import functools
import math

import jax
import jax.numpy as jnp
from jax import lax
from jax.experimental import pallas as pl
from jax.experimental.pallas import tpu as pltpu

F32 = jnp.float32
BF16 = jnp.bfloat16
NORM_EPS = 1e-6

LANES = 128
SUBLANES = 8
VMEM_LIMIT = 56 * 1024 * 1024

SC_KERNEL = 3
GDN_HEADS = 4
GDN_CONV = 4
GDN_CHUNK = 64
INV_BASE = 8
SB_HEADS = 4
SB_BLOCK = 128
FFN_CONV = 3
FFN_COLS = 256
SB_SKIP_LOG = -88.0

_NT = (((1,), (1,)), ((), ()))


def _rms(x, w):
    return x * lax.rsqrt(jnp.mean(x * x, axis=-1, keepdims=True) + NORM_EPS) * w


def _softplus(x):
    return jnp.maximum(x, 0.0) + jnp.log1p(jnp.exp(-jnp.abs(x)))


def _softplus_pos(x):
    return jnp.maximum(x, 0.0) + jnp.log(1.0 + jnp.exp(-jnp.abs(x)))


def _sigmoid(x):
    return 1.0 / (1.0 + jnp.exp(-x))


def _silu(x):
    return x * _sigmoid(x)


def _const_spec(shape):
    nd = len(shape)
    return pl.BlockSpec(shape, lambda *_: (0,) * nd, pipeline_mode=pl.Buffered(1))


def _prev_rows_map(rows_per_tile):
    step = rows_per_tile // SUBLANES
    return lambda i: (jnp.maximum(i * step - 1, 0), 0)


def _inproj_body(x_ref, wn_ref, wsc_ref, wg_ref, wsb_ref, wab_ref, wconv_ref, osc_ref, og_ref, osb_ref, oab_ref,
                 halo_ref, *, tm, tiles_per_seq, heads):
    i = pl.program_id(0)

    @pl.when(i == 0)
    def _():
        halo_ref[...] = jnp.zeros_like(halo_ref)

    h = _rms(x_ref[...], wn_ref[...]).astype(BF16)
    keep = i % tiles_per_seq > 0
    hd = LANES
    wconv = wconv_ref[...]
    def project(group):
        gcols = slice(group * heads * hd, (group + 1) * heads * hd)
        og_ref[:, gcols] = jnp.dot(h, wg_ref[:, gcols], preferred_element_type=F32)

    project(0)
    for s in range(3 * heads):
        cols = slice(s * hd, (s + 1) * hd)
        if s % heads == 0:
            project(s // heads + 1)
        raw = og_ref[:, cols]
        xp = jnp.concatenate([jnp.where(keep, halo_ref[:, cols], 0.0), raw], axis=0)
        halo_ref[:, cols] = raw[tm - SUBLANES:]
        conv = xp[SUBLANES:] * wconv[GDN_CONV - 1:GDN_CONV, cols]
        for k in range(1, GDN_CONV):
            conv = conv + xp[SUBLANES - k:SUBLANES - k + tm] * wconv[GDN_CONV - 1 - k:GDN_CONV - k, cols]
        a = _silu(conv)
        if s < 2 * heads:
            inv = lax.rsqrt(jnp.sum(a * a, axis=-1, keepdims=True) + NORM_EPS)
            a = a * (inv * (hd ** -0.5) if s < heads else inv)
        og_ref[:, cols] = a
    osc_ref[...] = jnp.dot(h, wsc_ref[...], preferred_element_type=F32)
    osb_ref[...] = jnp.dot(h, wsb_ref[...], preferred_element_type=F32).astype(BF16)
    oab_ref[...] = jnp.dot(h, wab_ref[...], preferred_element_type=F32)


def _inproj(x2, wn, wsc, wg, wsb, wab, wconv, seq, tm):
    T, D = x2.shape
    outs = (wsc.shape[1], wg.shape[1], wsb.shape[1], wab.shape[1])
    body = functools.partial(_inproj_body, tm=tm, tiles_per_seq=seq // tm, heads=GDN_HEADS)
    return pl.pallas_call(
        body,
        grid=(T // tm,),
        in_specs=[pl.BlockSpec((tm, D), lambda i: (i, 0)), _const_spec(wn.shape), _const_spec(wsc.shape),
                  _const_spec(wg.shape), _const_spec(wsb.shape), _const_spec(wab.shape), _const_spec(wconv.shape)],
        out_specs=[pl.BlockSpec((tm, n), lambda i: (i, 0)) for n in outs],
        out_shape=[jax.ShapeDtypeStruct((T, outs[0]), F32), jax.ShapeDtypeStruct((T, outs[1]), F32),
                   jax.ShapeDtypeStruct((T, outs[2]), BF16), jax.ShapeDtypeStruct((T, outs[3]), F32)],
        scratch_shapes=[pltpu.VMEM((SUBLANES, wconv.shape[1]), F32)],
        compiler_params=pltpu.CompilerParams(dimension_semantics=("arbitrary",), vmem_limit_bytes=VMEM_LIMIT),
        name="inproj",
    )(x2, wn, wsc, wg, wsb, wab, wconv)


def _gdn_body(g_ref, ab_ref, alog_ref, dtb_ref, wnorm_ref, y_ref, s_ref, *, lt, width, heads):
    hd = width // heads
    C = GDN_CHUNK
    nc = lt // C
    t = pl.program_id(1)

    @pl.when(t == 0)
    def _():
        s_ref[...] = jnp.zeros_like(s_ref)

    ri = lax.broadcasted_iota(jnp.int32, (C, C), 0)
    ci = lax.broadcasted_iota(jnp.int32, (C, C), 1)
    causal = ri >= ci
    strict = ri > ci
    eye = (ri == ci).astype(F32)
    ltri = causal.astype(BF16)
    neg_a = -jnp.exp(alog_ref[...])
    dtb = dtb_ref[...]
    wnorm = wnorm_ref[...]
    dot = functools.partial(jnp.dot, preferred_element_type=F32)

    tiles = [(c, h) for c in range(nc) for h in range(heads)]
    gate = []
    for c in range(nc):
        rows = slice(c * C, (c + 1) * C)
        ab = ab_ref[rows, :]
        g_all = neg_a * _softplus(ab + dtb)
        g1 = g_all.astype(BF16)
        r1 = g_all - g1.astype(F32)
        g2 = r1.astype(BF16)
        g3 = (r1 - g2.astype(F32)).astype(BF16)
        gc_all = dot(ltri, g1) + (dot(ltri, g2) + dot(ltri, g3))
        g_last = gc_all[C - 1:C, :]
        gate.append(dict(beta=_sigmoid(ab), gc=gc_all, gc_t=gc_all.T, eg=jnp.exp(gc_all),
                         ekd=jnp.exp(g_last - gc_all), eg_last=jnp.exp(g_last)))

    ks, qs, decays, kbs, rhss = [], [], [], [], []
    for c, h in tiles:
        gt = gate[c]
        rows = slice(c * C, (c + 1) * C)
        q = g_ref[rows, h * hd:(h + 1) * hd]
        k = g_ref[rows, (heads + h) * hd:(heads + h + 1) * hd]
        v = g_ref[rows, (2 * heads + h) * hd:(2 * heads + h + 1) * hd]
        beta = gt["beta"][:, heads + h:heads + h + 1]
        diff = gt["gc"][:, h:h + 1] - gt["gc_t"][h:h + 1, :]
        decays.append(jnp.where(causal, jnp.exp(jnp.where(causal, diff, 0.0)), 0.0))
        kb = k * beta
        ks.append(k)
        qs.append(q)
        kbs.append(kb)
        rhss.append(jnp.concatenate([v * beta, kb * gt["eg"][:, h:h + 1]], axis=1))
    kk = [lax.dot_general(kb.astype(BF16), k.astype(BF16), _NT, preferred_element_type=F32) for kb, k in zip(kbs, ks)]
    qk = [lax.dot_general(q.astype(BF16), k.astype(BF16), _NT, preferred_element_type=F32) for q, k in zip(qs, ks)]
    lms = [jnp.where(strict, x * d, 0.0) for x, d in zip(kk, decays)]
    attns = [(x * d).astype(BF16) for x, d in zip(qk, decays)]

    def same_block(n):
        shift = n.bit_length() - 1
        return jnp.right_shift(ri, shift) == jnp.right_shift(ci, shift)

    def bdot(a, b):
        return dot(a.astype(BF16), b.astype(BF16))

    inner = same_block(INV_BASE)
    ms = [jnp.where(inner, lm, 0.0) for lm in lms]
    ps = [eye - d for d in ms]
    span = 2
    while span < INV_BASE:
        ms = [bdot(m, m) for m in ms]
        ps = [p + bdot(p, m) for p, m in zip(ps, ms)]
        span *= 2
    size = INV_BASE
    while size < C:
        outer = same_block(2 * size)
        below = jnp.logical_and(outer, jnp.logical_not(inner))
        bt = [bdot(jnp.where(below, lm, 0.0), p) for lm, p in zip(lms, ps)]
        ps = [p - bdot(p, x) for p, x in zip(ps, bt)]
        inner = outer
        size *= 2
    sols = [bdot(p, r) for p, r in zip(ps, rhss)]

    for c in range(nc):
        gt = gate[c]
        rows = slice(c * C, (c + 1) * C)
        idx = [c * heads + h for h in range(heads)]
        states = [s_ref[h] for h in range(heads)]
        wq = [jnp.concatenate([sols[i][:, hd:], qs[i] * gt["eg"][:, h:h + 1]], axis=0).astype(BF16)
              for h, i in enumerate(idx)]
        kd_t = [(ks[i] * gt["ekd"][:, h:h + 1]).T.astype(BF16) for h, i in enumerate(idx)]
        wq_s = [dot(wq[h], states[h].astype(BF16)) for h in range(heads)]
        v_new = [(sols[i][:, :hd] - wq_s[h][:C]).astype(BF16) for h, i in enumerate(idx)]
        for h in range(heads):
            s_ref[h] = states[h] * gt["eg_last"][:, h:h + 1] + dot(kd_t[h], v_new[h])
        for h, i in enumerate(idx):
            o = wq_s[h][C:] + dot(attns[i], v_new[h])
            z = g_ref[rows, (3 * heads + h) * hd:(3 * heads + h + 1) * hd]
            y_ref[rows, h * hd:(h + 1) * hd] = (_rms(o, wnorm) * _silu(z)).astype(y_ref.dtype)


def _gdn(og, oab, alog, dtb, wnorm, batch, seq, lt):
    T, gw = og.shape
    width = gw // 4
    heads = GDN_HEADS
    nt = seq // lt
    body = functools.partial(_gdn_body, lt=lt, width=width, heads=heads)
    return pl.pallas_call(
        body,
        grid=(batch, nt),
        in_specs=[pl.BlockSpec((lt, gw), lambda b, t: (b * nt + t, 0)),
                  pl.BlockSpec((lt, LANES), lambda b, t: (b * nt + t, 0)),
                  pl.BlockSpec(alog.shape, lambda b, t: (0, 0)),
                  pl.BlockSpec(dtb.shape, lambda b, t: (0, 0)),
                  pl.BlockSpec(wnorm.shape, lambda b, t: (0, 0))],
        out_specs=pl.BlockSpec((lt, width), lambda b, t: (b * nt + t, 0)),
        out_shape=jax.ShapeDtypeStruct((T, width), BF16),
        scratch_shapes=[pltpu.VMEM((heads, width // heads, width // heads), F32)],
        compiler_params=pltpu.CompilerParams(dimension_semantics=("arbitrary", "arbitrary"),
                                             vmem_limit_bytes=VMEM_LIMIT),
        name="gdn",
    )(og, oab, alog, dtb, wnorm)


def _sb_body(q_ref, k_ref, v_ref, o_ref, *, nsub, hd):
    bq = SB_BLOCK
    step = pl.program_id(2)
    scale = hd ** -0.5
    scale_on_q = math.frexp(scale)[0] == 0.5
    lane = lax.broadcasted_iota(jnp.int32, (1, LANES), 1)
    first_head = lane < hd

    def later(n):
        return (lax.broadcasted_iota(jnp.int32, (n, n), 0) > lax.broadcasted_iota(jnp.int32, (n, n), 1)).astype(BF16)

    def keys_after(lom, later_m):
        hi = lom.astype(BF16)
        lo = (lom - hi.astype(F32)).astype(BF16)
        return jnp.dot(hi, later_m, preferred_element_type=F32) + jnp.dot(lo, later_m, preferred_element_type=F32)

    def stack_heads(q):
        if scale_on_q:
            q = q * jnp.asarray(scale, q.dtype)
        zero = jnp.zeros_like(q)
        return jnp.concatenate([jnp.where(first_head, q, zero), jnp.where(first_head, zero, q)], axis=0)

    row = lax.broadcasted_iota(jnp.int32, (2 * bq, 2 * bq), 0)
    ahead = lax.broadcasted_iota(jnp.int32, (2 * bq, 2 * bq), 1) - jnp.where(row >= bq, row - bq, row)
    later2 = later(2 * bq)

    def logits(qq, rows):
        z = lax.dot_general(qq, k_ref[rows, :], _NT, preferred_element_type=F32)
        return z if scale_on_q else z * scale

    blocks = [step * nsub + s for s in range(nsub)]
    starts = [jnp.maximum(i - 1, 0) * bq for i in blocks]
    qqs = [stack_heads(q_ref[s * bq:(s + 1) * bq, :]) for s in range(nsub)]
    wins = [pl.ds(pl.multiple_of(w, bq), 2 * bq) for w in starts]
    zs = [logits(qq, win) for qq, win in zip(qqs, wins)]
    stricts = [ahead < i * bq - w for i, w in zip(blocks, starts)]
    loms = [jnp.where(st, -_softplus_pos(z), 0.0) for st, z in zip(stricts, zs)]
    tails = [keys_after(lom, later2) for lom in loms]
    weights = [jnp.where(st, jnp.exp(z + lom + tail), 0.0).astype(BF16)
               for st, z, lom, tail in zip(stricts, zs, loms, tails)]
    accs = [jnp.dot(a, v_ref[win, :], preferred_element_type=F32) for a, win in zip(weights, wins)]
    carries = [jnp.sum(lom, axis=-1, keepdims=True) for lom in loms]

    later1 = later(bq)

    def cond(state):
        j, carry, _ = state
        return jnp.logical_and(j >= 0, jnp.max(carry) > SB_SKIP_LOG)

    for s in range(nsub):
        qq = qqs[s]

        def body(state, qq=qq):
            j, carry, acc = state
            rows = pl.ds(pl.multiple_of(j * bq, bq), bq)
            z = logits(qq, rows)
            lom = -_softplus_pos(z)
            a = jnp.exp(z + lom + keys_after(lom, later1) + carry)
            acc = acc + jnp.dot(a.astype(BF16), v_ref[rows, :], preferred_element_type=F32)
            return j - 1, carry + jnp.sum(lom, axis=-1, keepdims=True), acc

        _, _, acc = lax.while_loop(cond, body, (blocks[s] - 2, carries[s], accs[s]))
        o_ref[s * bq:(s + 1) * bq, :] = jnp.where(first_head, acc[:bq], acc[bq:]).astype(o_ref.dtype)


def _sb(osb, batch, seq, nsub):
    T, w3 = osb.shape
    width = w3 // 3
    pairs = width // LANES
    hd = width // SB_HEADS
    tq = nsub * SB_BLOCK
    nq = seq // tq
    assert seq % tq == 0 and seq >= 2 * SB_BLOCK
    body = functools.partial(_sb_body, nsub=nsub, hd=hd)
    return pl.pallas_call(
        body,
        grid=(batch, pairs, nq),
        in_specs=[pl.BlockSpec((tq, LANES), lambda b, p, i: (b * nq + i, p)),
                  pl.BlockSpec((seq, LANES), lambda b, p, i: (b, pairs + p)),
                  pl.BlockSpec((seq, LANES), lambda b, p, i: (b, 2 * pairs + p))],
        out_specs=pl.BlockSpec((tq, LANES), lambda b, p, i: (b * nq + i, p)),
        out_shape=jax.ShapeDtypeStruct((T, width), BF16),
        compiler_params=pltpu.CompilerParams(dimension_semantics=("arbitrary", "arbitrary", "arbitrary"),
                                             vmem_limit_bytes=VMEM_LIMIT),
        name="stickbreak",
    )(osb, osb, osb)


def _mixffn_body(x_ref, sc_ref, scprev_ref, yg_ref, ysb_ref, wsconv_ref, wa_ref, wb_ref, wc_ref,
                 wn_ref, wup_ref, wconv_ref, wdown_ref, wfin_ref, o_ref, act_ref, halo_ref,
                 *, tm, tiles_per_seq, scw, dff, cw, final_norm):
    i = pl.program_id(0)

    @pl.when(i == 0)
    def _():
        halo_ref[...] = jnp.zeros_like(halo_ref)

    keep = i % tiles_per_seq > 0

    def causal_conv(u, w, taps):
        out = u[SUBLANES:] * w[taps - 1:taps]
        for k in range(1, taps):
            out = out + u[SUBLANES - k:SUBLANES - k + tm] * w[taps - 1 - k:taps - k]
        return out

    p = sc_ref[:, scw:2 * scw] * sc_ref[:, 2 * scw:3 * scw]
    pprev = jnp.where(keep, scprev_ref[:, scw:2 * scw] * scprev_ref[:, 2 * scw:3 * scw], 0.0)
    ysc = sc_ref[:, 0:scw] * causal_conv(jnp.concatenate([pprev, p], axis=0), wsconv_ref[...], SC_KERNEL)
    xn = x_ref[...] + jnp.dot(ysc.astype(BF16), wa_ref[...], preferred_element_type=F32)
    xn = xn + jnp.dot(yg_ref[...], wb_ref[...], preferred_element_type=F32)
    xn = xn + jnp.dot(ysb_ref[...], wc_ref[...], preferred_element_type=F32)

    h = _rms(xn, wn_ref[...]).astype(BF16)

    def up_conv(cols):
        u = jnp.dot(h, wup_ref[:, cols], preferred_element_type=F32)
        above = jnp.where(keep, halo_ref[:, cols], 0.0)
        halo_ref[:, cols] = u[tm - SUBLANES:]
        return causal_conv(jnp.concatenate([above, u], axis=0), wconv_ref[:, cols], FFN_CONV)

    for j in range(dff // cw):
        gcols = slice(j * cw, (j + 1) * cw)
        gate = up_conv(gcols)
        val = up_conv(slice(dff + j * cw, dff + (j + 1) * cw))
        act_ref[:, gcols] = (_silu(gate) * val).astype(BF16)
    y = xn + jnp.dot(act_ref[...], wdown_ref[...], preferred_element_type=F32)
    if final_norm:
        y = _rms(y, wfin_ref[...])
    o_ref[...] = y


def _mixffn(x2, osc, yg, ysb, wsconv, wa, wb, wc, wn, wup, wconv, wdown, wfin, seq, tm, final_norm):
    T, D = x2.shape
    scw = wa.shape[0]
    dff = wdown.shape[0]
    body = functools.partial(_mixffn_body, tm=tm, tiles_per_seq=seq // tm, scw=scw, dff=dff, cw=FFN_COLS,
                             final_norm=final_norm)
    row = lambda n: pl.BlockSpec((tm, n), lambda i: (i, 0))
    consts = (wsconv, wa, wb, wc, wn, wup, wconv, wdown, wfin)
    return pl.pallas_call(
        body,
        grid=(T // tm,),
        in_specs=[row(D), row(osc.shape[1]), pl.BlockSpec((SUBLANES, osc.shape[1]), _prev_rows_map(tm)),
                  row(yg.shape[1]), row(ysb.shape[1])] + [_const_spec(c.shape) for c in consts],
        out_specs=row(D),
        out_shape=jax.ShapeDtypeStruct((T, D), F32),
        scratch_shapes=[pltpu.VMEM((tm, dff), BF16), pltpu.VMEM((SUBLANES, 2 * dff), F32)],
        compiler_params=pltpu.CompilerParams(dimension_semantics=("arbitrary",), vmem_limit_bytes=VMEM_LIMIT),
        name="mixffn",
    )(x2, osc, osc, yg, ysb, *consts)


def _pad_lanes(a):
    return jnp.pad(a, ((0, 0), (0, LANES - a.shape[1])))


def kernel(x, w_norm_mix, w_mix_in, w_sconv, w_gdn_conv, gdn_a_log, gdn_dt_bias, w_gdn_norm, w_mix_out,
           w_norm_ffn, w_ffn_up, w_ffn_conv, w_ffn_down, w_norm_final):
    batch, seq, D = x.shape
    depth = w_mix_in.shape[0]
    scw = D // 4
    gw = D // 2
    sbw = D - scw - gw
    c_g = 3 * scw
    c_ab = c_g + 4 * gw
    c_sb = c_ab + 2 * GDN_HEADS
    assert w_mix_in.shape[2] == c_sb + 3 * sbw and gw // GDN_HEADS == LANES and sbw % LANES == 0

    x2 = x.reshape(batch * seq, D).astype(F32)
    row = lambda v: v.reshape(1, -1).astype(F32)
    for l in range(depth):
        w_in = w_mix_in[l]
        wsc = w_in[:, :c_g].astype(BF16)
        wg = w_in[:, c_g:c_ab].astype(BF16)
        wab = _pad_lanes(w_in[:, c_ab:c_sb]).astype(BF16)
        wsb = w_in[:, c_sb:].astype(BF16)
        osc, og, osb, oab = _inproj(x2, row(w_norm_mix[l]), wsc, wg, wsb, wab, w_gdn_conv[l].astype(F32), seq, tm=1024)

        yg = _gdn(og, oab, _pad_lanes(row(gdn_a_log[l])), _pad_lanes(row(gdn_dt_bias[l])),
                  row(w_gdn_norm[l]), batch, seq, lt=1024)
        ysb = _sb(osb, batch, seq, nsub=16)

        w_out = w_mix_out[l].astype(BF16)
        x2 = _mixffn(x2, osc, yg, ysb, w_sconv[l].astype(F32), w_out[:scw], w_out[scw:scw + gw], w_out[scw + gw:],
                     row(w_norm_ffn[l]), w_ffn_up[l].astype(BF16), w_ffn_conv[l].astype(F32),
                     w_ffn_down[l].astype(BF16), row(w_norm_final), seq, tm=512, final_norm=(l == depth - 1))
    return x2.reshape(batch, seq, D).astype(x.dtype)
```
